```python
import math
import jax, jax.numpy as jnp
from jax import lax
import numpy as np

D_MODEL = 1024
BATCH = 4
SEQ = 8192
DEPTH = 2
DEC_BATCH = 2
DEC_SEQ = 16384
PAST_LEN = 128

N_MIXERS = 2
DA_HEADS = 8
DA_HEAD_DIM = 64
DL_HEADS = 16
DL_HEAD_DIM = 64
DL_PATTERNS = ((128, 1), (512, 4), (2048, 16))
D_FF = 2816
CONV_WIDTH = 3
ROPE_THETA = 10000.0
NORM_EPS = 1e-6
Q_BLOCK = 128
NEG = -1e30
N_LAYERS_A = (DEPTH + 1) // 2
N_LAYERS_B = DEPTH // 2

kernel_name = 'hybrid_diffattn_dilatedswa_convffn_encoder'


def rms_norm(x, g):
    xf = x.astype(jnp.float32)
    y = xf * lax.rsqrt(jnp.mean(xf * xf, axis=-1, keepdims=True) + NORM_EPS)
    return (y * g.astype(jnp.float32)).astype(x.dtype)


def rope_tables(seq, dim):
    inv = 1.0 / (ROPE_THETA ** (jnp.arange(0, dim, 2, dtype=jnp.float32) / dim))
    ang = jnp.arange(seq, dtype=jnp.float32)[:, None] * inv[None, :]
    return jnp.cos(ang), jnp.sin(ang)


def apply_rope(x, cos, sin):
    x1, x2 = jnp.split(x.astype(jnp.float32), 2, axis=-1)
    c = cos[None, :, None, :]
    s = sin[None, :, None, :]
    return jnp.concatenate([x1 * c - x2 * s, x1 * s + x2 * c], axis=-1).astype(x.dtype)


def diff_attention(x, w_in, w_out, lq1, lk1, lq2, lk2, subln_g, lambda_init, cos, sin):
    B, S, _ = x.shape
    H, Dh = DA_HEADS, DA_HEAD_DIM
    q, k, v = jnp.split(x @ w_in, 3, axis=-1)
    q = apply_rope(q.reshape(B, S, 2 * H, Dh), cos, sin).reshape(B, S, H, 2, Dh)
    k = apply_rope(k.reshape(B, S, 2 * H, Dh), cos, sin).reshape(B, S, H, 2, Dh)
    v = v.reshape(B, S, H, 2 * Dh)
    f32 = jnp.float32
    lam = (jnp.exp(jnp.sum(lq1.astype(f32) * lk1.astype(f32)))
           - jnp.exp(jnp.sum(lq2.astype(f32) * lk2.astype(f32))) + lambda_init)
    scale = Dh ** -0.5
    nblk = S // Q_BLOCK
    qb = q.reshape(B, nblk, Q_BLOCK, H, 2, Dh).transpose(1, 0, 2, 3, 4, 5)

    def block(qblk):
        s = jnp.einsum('bqhcd,bkhcd->bchqk', qblk, k).astype(f32) * scale
        p = jax.nn.softmax(s, axis=-1)
        a = p[:, 0] - lam * p[:, 1]
        return jnp.einsum('bhqk,bkhe->bqhe', a.astype(v.dtype), v)

    o = lax.map(block, qb)
    o = o.transpose(1, 0, 2, 3, 4).reshape(B, S, H, 2 * Dh)
    o = rms_norm(o, subln_g) * (1.0 - lambda_init)
    return o.reshape(B, S, H * 2 * Dh) @ w_out


def dilated_group(q, k, v, dil, half):
    B, S, H, Dh = q.shape
    E = v.shape[-1]
    L = S // dil
    nb = -(-L // half)
    Lp = nb * half

    def to_sub(t, lo, hi):
        t = t.reshape(B, L, dil, H, t.shape[-1]).transpose(0, 2, 1, 3, 4)
        return jnp.pad(t, ((0, 0), (0, 0), (lo, hi), (0, 0), (0, 0)))

    def band(t):
        e = t.shape[-1]
        t = to_sub(t, half, Lp - L + half).reshape(B, dil, nb + 2, half, H, e)
        return jnp.concatenate([t[:, :, 0:nb], t[:, :, 1:nb + 1], t[:, :, 2:nb + 2]], axis=3)

    qs = to_sub(q, 0, Lp - L).reshape(B, dil, nb, half, H, Dh)
    kb = band(k)
    vb = band(v)
    qi = jnp.arange(half)[:, None]
    kj = jnp.arange(3 * half)[None, :]
    kpos = (jnp.arange(nb)[:, None] - 1) * half + jnp.arange(3 * half)[None, :]
    valid = (jnp.abs(kj - half - qi) <= half)[None] & ((kpos >= 0) & (kpos < L))[:, None, :]
    s = jnp.einsum('bgnqhd,bgnkhd->bgnhqk', qs, kb).astype(jnp.float32) * (Dh ** -0.5)
    s = jnp.where(valid[None, None, :, None], s, NEG)
    m = jnp.max(s, axis=-1, keepdims=True)
    p = jnp.exp(s - m)
    l = jnp.sum(p, axis=-1, keepdims=True)
    o = jnp.einsum('bgnhqk,bgnkhe->bgnqhe', p, vb.astype(jnp.float32))
    o = o / l.transpose(0, 1, 2, 4, 3, 5)
    lse = (m + jnp.log(l))[..., 0].transpose(0, 1, 2, 4, 3)
    o = o.reshape(B, dil, Lp, H, E)[:, :, :L].transpose(0, 2, 1, 3, 4).reshape(B, S, H, E)
    lse = lse.reshape(B, dil, Lp, H)[:, :, :L].transpose(0, 2, 1, 3).reshape(B, S, H)
    return o, lse


def dilated_attention(x, w_in, w_out, cos, sin):
    B, S, _ = x.shape
    G = len(DL_PATTERNS)
    H, Dh = DL_HEADS, DL_HEAD_DIM
    qg, kg, v = jnp.split(x @ w_in, [G * H * Dh, 2 * G * H * Dh], axis=-1)
    qg = apply_rope(qg.reshape(B, S, G * H, Dh), cos, sin).reshape(B, S, G, H, Dh)
    kg = apply_rope(kg.reshape(B, S, G * H, Dh), cos, sin).reshape(B, S, G, H, Dh)
    v = v.reshape(B, S, H, Dh)
    outs, lses = [], []
    for g, (window, dil) in enumerate(DL_PATTERNS):
        o_g, lse_g = dilated_group(qg[:, :, g], kg[:, :, g], v, dil, window // (2 * dil))
        outs.append(o_g)
        lses.append(lse_g)
    alpha = jax.nn.softmax(jnp.stack(lses, axis=0), axis=0)
    o = jnp.einsum('gbsh,gbshe->bshe', alpha, jnp.stack(outs, axis=0))
    return o.astype(x.dtype).reshape(B, S, H * Dh) @ w_out


def conv_ffn(x, w_up, conv_w, conv_b, w_down):
    h = x @ w_up
    C = h.shape[-1]
    h = lax.conv_general_dilated(h, conv_w[:, None, :].astype(h.dtype), window_strides=(1,),
                                 padding=((CONV_WIDTH // 2, CONV_WIDTH // 2),),
                                 dimension_numbers=('NWC', 'WIO', 'NWC'),
                                 feature_group_count=C) + conv_b
    gate, up = jnp.split(h, 2, axis=-1)
    return (jax.nn.gelu(gate, approximate=True) * up) @ w_down


def trunk(x, norm_mix_pre, norm_mix_post, norm_ffn_pre, norm_ffn_post,
          da_w_in, da_w_out, da_lambda_q1, da_lambda_k1, da_lambda_q2, da_lambda_k2, da_subln,
          dl_w_in, dl_w_out, ffn_w_up, ffn_conv_w, ffn_conv_b, ffn_w_down):
    S = x.shape[1]
    cos, sin = rope_tables(S, DA_HEAD_DIM)
    for i in range(DEPTH):
        j = i // N_MIXERS
        h = rms_norm(x, norm_mix_pre[i])
        if i % N_MIXERS == 0:
            lambda_init = 0.8 - 0.6 * math.exp(-0.3 * i)
            h = diff_attention(h, da_w_in[j], da_w_out[j], da_lambda_q1[j], da_lambda_k1[j],
                               da_lambda_q2[j], da_lambda_k2[j], da_subln[j], lambda_init, cos, sin)
        else:
            h = dilated_attention(h, dl_w_in[j], dl_w_out[j], cos, sin)
        x = x + rms_norm(h, norm_mix_post[i])
        h = conv_ffn(rms_norm(x, norm_ffn_pre[i]), ffn_w_up[i], ffn_conv_w[i], ffn_conv_b[i], ffn_w_down[i])
        x = x + rms_norm(h, norm_ffn_post[i])
    return x


def setup_inputs(seed: int = 0) -> dict:
    key = jax.random.key(seed)
    ks = jax.random.split(key, 20)
    f32 = jnp.float32
    D = D_MODEL
    G = len(DL_PATTERNS)
    da_in = 3 * 2 * DA_HEADS * DA_HEAD_DIM
    da_o = DA_HEADS * 2 * DA_HEAD_DIM
    dl_in = (2 * G + 1) * DL_HEADS * DL_HEAD_DIM
    dl_o = DL_HEADS * DL_HEAD_DIM

    def nrm(k, shape, scale):
        return jax.random.normal(k, shape, f32) * scale

    return {
        'x_prompt': nrm(ks[0], (BATCH, SEQ, D), 1.0),
        'x_sample': nrm(ks[1], (DEC_BATCH, DEC_SEQ, D), 1.0),
        'norm_mix_pre': 1.0 + nrm(ks[2], (DEPTH, D), 0.05),
        'norm_mix_post': 1.0 + nrm(ks[3], (DEPTH, D), 0.05),
        'norm_ffn_pre': 1.0 + nrm(ks[4], (DEPTH, D), 0.05),
        'norm_ffn_post': 1.0 + nrm(ks[5], (DEPTH, D), 0.05),
        'da_w_in': nrm(ks[6], (N_LAYERS_A, D, da_in), D ** -0.5),
        'da_w_out': nrm(ks[7], (N_LAYERS_A, da_o, D), da_o ** -0.5),
        'da_lambda_q1': nrm(ks[8], (N_LAYERS_A, DA_HEAD_DIM), 0.1),
        'da_lambda_k1': nrm(ks[9], (N_LAYERS_A, DA_HEAD_DIM), 0.1),
        'da_lambda_q2': nrm(ks[10], (N_LAYERS_A, DA_HEAD_DIM), 0.1),
        'da_lambda_k2': nrm(ks[11], (N_LAYERS_A, DA_HEAD_DIM), 0.1),
        'da_subln': 1.0 + nrm(ks[12], (N_LAYERS_A, 2 * DA_HEAD_DIM), 0.05),
        'dl_w_in': nrm(ks[13], (N_LAYERS_B, D, dl_in), D ** -0.5),
        'dl_w_out': nrm(ks[14], (N_LAYERS_B, dl_o, D), dl_o ** -0.5),
        'ffn_w_up': nrm(ks[15], (DEPTH, D, 2 * D_FF), D ** -0.5),
        'ffn_conv_w': nrm(ks[16], (DEPTH, CONV_WIDTH, 2 * D_FF), CONV_WIDTH ** -0.5),
        'ffn_conv_b': nrm(ks[17], (DEPTH, 2 * D_FF), 0.01),
        'ffn_w_down': nrm(ks[18], (DEPTH, D_FF, D), D_FF ** -0.5),
    }


def reference(x_prompt, x_sample, norm_mix_pre, norm_mix_post, norm_ffn_pre, norm_ffn_post,
              da_w_in, da_w_out, da_lambda_q1, da_lambda_k1, da_lambda_q2, da_lambda_k2, da_subln,
              dl_w_in, dl_w_out, ffn_w_up, ffn_conv_w, ffn_conv_b, ffn_w_down):
    y_prompt = trunk(x_prompt, norm_mix_pre, norm_mix_post, norm_ffn_pre, norm_ffn_post,
                     da_w_in, da_w_out, da_lambda_q1, da_lambda_k1, da_lambda_q2, da_lambda_k2, da_subln,
                     dl_w_in, dl_w_out, ffn_w_up, ffn_conv_w, ffn_conv_b, ffn_w_down)
    y_sample = trunk(x_sample, norm_mix_pre, norm_mix_post, norm_ffn_pre, norm_ffn_post,
                     da_w_in, da_w_out, da_lambda_q1, da_lambda_k1, da_lambda_q2, da_lambda_k2, da_subln,
                     dl_w_in, dl_w_out, ffn_w_up, ffn_conv_w, ffn_conv_b, ffn_w_down)
    return (y_prompt, y_sample)
```

```python
import functools
import math

import numpy as np
import jax
import jax.numpy as jnp
from jax import lax
from jax.experimental import pallas as pl
from jax.experimental.pallas import tpu as pltpu

D_MODEL = 1024
DEPTH = 2
N_MIXERS = 2
DA_HEADS = 8
HEAD_DIM = 64
DL_HEADS = 16
DL_PATTERNS = ((128, 1), (512, 4), (2048, 16))
D_FF = 2816
ROPE_THETA = 10000.0
NORM_EPS = 1e-6
NEG = -1e30

LANES = 128
PAIR = 2 * HEAD_DIM
VMEM_LIMIT = 56 * 1024 * 1024
Q_SCALE = (HEAD_DIM ** -0.5) * math.log2(math.e)

TM = 512
TQ = 512
TK = 512
FF_CHUNK = 256
DIL_TQ = 1024
DIL_SUB = 128
DIL_HALF = 64

BF16 = jnp.bfloat16
F32 = jnp.float32


def _cparams(sem):
    return pltpu.CompilerParams(dimension_semantics=sem, vmem_limit_bytes=VMEM_LIMIT)


def _const_spec(shape):
    nd = len(shape)
    return pl.BlockSpec(shape, lambda *_: (0,) * nd, pipeline_mode=pl.Buffered(1))


def _rms(x, g):
    ms = jnp.mean(x * x, axis=-1, keepdims=True)
    return x * lax.rsqrt(ms + NORM_EPS) * g


def _rope_perm(n):
    return np.arange(n).reshape(n // PAIR, 2, 2, HEAD_DIM // 2).transpose(0, 2, 1, 3).reshape(-1)


def _qkv_a_kernel(x_ref, g_ref, wk_ref, wqv_ref, ck_ref, sk_ref, ct_ref, st_ref,
                  k_ref, qt_ref, vt_ref):
    xn = _rms(x_ref[...], g_ref[...]).astype(BF16)
    k = jnp.dot(xn, wk_ref[...], preferred_element_type=F32)
    ck = ck_ref[...]
    sk = sk_ref[...]
    for h in range(DA_HEADS):
        kh = k[:, h * PAIR:(h + 1) * PAIR]
        k_ref[:, h * PAIR:(h + 1) * PAIR] = (kh * ck + pltpu.roll(kh, HEAD_DIM, 1) * sk).astype(BF16)
    qvt = lax.dot_general(wqv_ref[...], xn, (((1,), (1,)), ((), ())), preferred_element_type=F32)
    ct = ct_ref[...]
    st = st_ref[...]
    for h in range(DA_HEADS):
        x1 = qvt[h * PAIR:h * PAIR + HEAD_DIM]
        x2 = qvt[h * PAIR + HEAD_DIM:(h + 1) * PAIR]
        qt_ref[h * PAIR:h * PAIR + HEAD_DIM, :] = ((x1 * ct - x2 * st) * Q_SCALE).astype(BF16)
        qt_ref[h * PAIR + HEAD_DIM:(h + 1) * PAIR, :] = ((x1 * st + x2 * ct) * Q_SCALE).astype(BF16)
    vt_ref[...] = qvt[D_MODEL:].astype(BF16)


def _qkv_a(x, g, wk, wqv_t, ck, sk, ct, st):
    B, S, D = x.shape
    tm = min(TM, S)
    nqk = DA_HEADS * PAIR
    return pl.pallas_call(
        _qkv_a_kernel,
        grid=(B, S // tm),
        in_specs=[
            pl.BlockSpec((None, tm, D), lambda b, i: (b, i, 0)),
            _const_spec((1, D)),
            _const_spec((D, nqk)),
            _const_spec((2 * nqk, D)),
            pl.BlockSpec((tm, PAIR), lambda b, i: (i, 0)),
            pl.BlockSpec((tm, PAIR), lambda b, i: (i, 0)),
            pl.BlockSpec((HEAD_DIM, tm), lambda b, i: (0, i)),
            pl.BlockSpec((HEAD_DIM, tm), lambda b, i: (0, i)),
        ],
        out_specs=[
            pl.BlockSpec((None, tm, nqk), lambda b, i: (b, i, 0)),
            pl.BlockSpec((None, nqk, tm), lambda b, i: (b, 0, i)),
            pl.BlockSpec((None, nqk, tm), lambda b, i: (b, 0, i)),
        ],
        out_shape=[
            jax.ShapeDtypeStruct((B, S, nqk), BF16),
            jax.ShapeDtypeStruct((B, nqk, S), BF16),
            jax.ShapeDtypeStruct((B, nqk, S), BF16),
        ],
        compiler_params=_cparams(("parallel", "parallel")),
        name="qkv_a",
    )(x, g, wk, wqv_t, ck, sk, ct, st)


def _diff_attn_kernel(qt_ref, k_ref, vt_ref, lq1_ref, lk1_ref, lq2_ref, lk2_ref, g_ref, o_ref,
                      m_ref, l_ref, acc_ref, *, tk, lambda_init):
    S = k_ref.shape[0]
    tq = qt_ref.shape[1]
    qt = qt_ref[...]
    row = lax.broadcasted_iota(jnp.int32, qt.shape, 0)
    comp0 = (row % HEAD_DIM) < (HEAD_DIM // 2)
    zero = jnp.zeros_like(qt)
    qts = (jnp.where(comp0, qt, zero), jnp.where(comp0, zero, qt))

    m_ref[...] = jnp.full(m_ref.shape, NEG, F32)
    l_ref[...] = jnp.zeros(l_ref.shape, F32)
    acc_ref[...] = jnp.zeros(acc_ref.shape, F32)

    def body(j, carry):
        start = pl.multiple_of(j * tk, tk)
        kb = k_ref[pl.ds(start, tk), :]
        vtb = vt_ref[:, pl.ds(start, tk)]
        for c in range(2):
            s = jnp.dot(kb, qts[c], preferred_element_type=F32)
            m_old = m_ref[c:c + 1, :]
            m_new = jnp.maximum(m_old, jnp.max(s, axis=0, keepdims=True))
            alpha = jnp.exp2(m_old - m_new)
            p = jnp.exp2(s - m_new)
            l_ref[c:c + 1, :] = alpha * l_ref[c:c + 1, :] + jnp.sum(p, axis=0, keepdims=True)
            acc_ref[c] = alpha * acc_ref[c] + jnp.dot(vtb, p.astype(BF16), preferred_element_type=F32)
            m_ref[c:c + 1, :] = m_new
        return carry

    lax.fori_loop(0, S // tk, body, 0)

    lam = (jnp.exp(jnp.sum(lq1_ref[...] * lk1_ref[...], axis=-1, keepdims=True))
           - jnp.exp(jnp.sum(lq2_ref[...] * lk2_ref[...], axis=-1, keepdims=True)) + lambda_init)
    o0 = acc_ref[0] * (1.0 / l_ref[0:1, :])
    o1 = acc_ref[1] * (1.0 / l_ref[1:2, :])
    ot = o0 - lam * o1
    ms = jnp.mean(ot * ot, axis=0, keepdims=True)
    on = (ot * lax.rsqrt(ms + NORM_EPS)).T
    o_ref[...] = (on * g_ref[...] * (1.0 - lambda_init)).astype(o_ref.dtype)


def _diff_attn(qt, k, vt, lq1, lk1, lq2, lk2, subln_g, lambda_init):
    B, nqk, S = qt.shape
    tq = min(TQ, S)
    tk = min(TK, S)
    lam_spec = _const_spec((1, HEAD_DIM))
    return pl.pallas_call(
        functools.partial(_diff_attn_kernel, tk=tk, lambda_init=lambda_init),
        grid=(B, DA_HEADS, S // tq),
        in_specs=[
            pl.BlockSpec((None, PAIR, tq), lambda b, h, i: (b, h, i)),
            pl.BlockSpec((None, S, PAIR), lambda b, h, i: (b, 0, h)),
            pl.BlockSpec((None, PAIR, S), lambda b, h, i: (b, h, 0)),
            lam_spec, lam_spec, lam_spec, lam_spec,
            _const_spec((1, PAIR)),
        ],
        out_specs=pl.BlockSpec((None, tq, PAIR), lambda b, h, i: (b, i, h)),
        out_shape=jax.ShapeDtypeStruct((B, S, nqk), BF16),
        scratch_shapes=[
            pltpu.VMEM((8, tq), F32),
            pltpu.VMEM((8, tq), F32),
            pltpu.VMEM((2, PAIR, tq), F32),
        ],
        compiler_params=_cparams(("parallel", "parallel", "arbitrary")),
        name="diff_attn",
    )(qt, k, vt, lq1, lk1, lq2, lk2, subln_g)


def _out_proj_kernel(o_ref, w_ref, g_ref, x_ref, y_ref):
    h = jnp.dot(o_ref[...], w_ref[...], preferred_element_type=F32)
    y_ref[...] = x_ref[...] + _rms(h, g_ref[...])


def _out_proj(o, w, g, x):
    B, S, D = x.shape
    tm = min(TM, S)
    tok = pl.BlockSpec((None, tm, D), lambda b, i: (b, i, 0))
    return pl.pallas_call(
        _out_proj_kernel,
        grid=(B, S // tm),
        in_specs=[tok, _const_spec((D, D)), _const_spec((1, D)), tok],
        out_specs=tok,
        out_shape=jax.ShapeDtypeStruct((B, S, D), F32),
        compiler_params=_cparams(("parallel", "parallel")),
        name="out_proj",
    )(o, w, g, x)


def _combine_out_proj_kernel(o0_ref, o1_ref, o2_ref, e0_ref, e1_ref, e2_ref, w_ref, g_ref, x_ref, y_ref):
    e0, e1, e2 = e0_ref[...], e1_ref[...], e2_ref[...]
    m = jnp.maximum(jnp.maximum(e0, e1), e2)
    a0, a1, a2 = jnp.exp2(e0 - m), jnp.exp2(e1 - m), jnp.exp2(e2 - m)
    o = (a0 * o0_ref[...].astype(F32) + a1 * o1_ref[...].astype(F32) + a2 * o2_ref[...].astype(F32))
    o = o * (1.0 / (a0 + a1 + a2))
    h = jnp.dot(o.astype(BF16), w_ref[...], preferred_element_type=F32)
    y_ref[...] = x_ref[...] + _rms(h, g_ref[...])


def _combine_out_proj(os, es, w, g, x):
    B, S, D = x.shape
    tm = min(TM, S)
    tok = pl.BlockSpec((None, tm, D), lambda b, i: (b, i, 0))
    return pl.pallas_call(
        _combine_out_proj_kernel,
        grid=(B, S // tm),
        in_specs=[tok] * 6 + [_const_spec((D, D)), _const_spec((1, D)), tok],
        out_specs=tok,
        out_shape=jax.ShapeDtypeStruct((B, S, D), F32),
        compiler_params=_cparams(("parallel", "parallel")),
        name="combine_out_proj",
    )(*os, *es, w, g, x)


def _ffn_kernel(x_ref, xp_ref, xn_ref, gpre_ref, wup_ref, cw_ref, cb_ref, wdn_ref, gpost_ref, y_ref,
                h_ref, acc_ref):
    i = pl.program_id(1)
    last = pl.num_programs(1) - 1
    tm = x_ref.shape[0]
    x = x_ref[...]
    gpre = gpre_ref[...]
    xe = jnp.concatenate([_rms(xp_ref[...], gpre), _rms(x, gpre), _rms(xn_ref[...], gpre)], axis=0).astype(BF16)
    acc_ref[...] = jnp.zeros(acc_ref.shape, F32)
    n_chunks = wup_ref.shape[0]

    def body(c, carry):
        h_ref[...] = jnp.dot(xe, wup_ref[c], preferred_element_type=F32)

        @pl.when(i == 0)
        def _():
            h_ref[7:8, :] = jnp.zeros((1, h_ref.shape[1]), F32)

        @pl.when(i == last)
        def _():
            h_ref[8 + tm:9 + tm, :] = jnp.zeros((1, h_ref.shape[1]), F32)

        cw = cw_ref[c]
        y = (h_ref[pl.ds(7, tm), :] * cw[0:1] + h_ref[pl.ds(8, tm), :] * cw[1:2]
             + h_ref[pl.ds(9, tm), :] * cw[2:3] + cb_ref[c])
        gate = y[:, :FF_CHUNK]
        up = y[:, FF_CHUNK:]
        cdf = 0.5 * (1.0 + jnp.tanh(math.sqrt(2.0 / math.pi) * (gate + 0.044715 * (gate * gate * gate))))
        act = (gate * cdf * up).astype(BF16)
        acc_ref[...] += jnp.dot(act, wdn_ref[c], preferred_element_type=F32)
        return carry

    lax.fori_loop(0, n_chunks, body, 0)
    y_ref[...] = x + _rms(acc_ref[...], gpost_ref[...])


def _ffn(x, gpre, wup, cw, cb, wdn, gpost):
    B, S, D = x.shape
    tm = min(TM, S)
    nt8 = S // 8
    r8 = tm // 8
    nc = wup.shape[0]
    return pl.pallas_call(
        _ffn_kernel,
        grid=(B, S // tm),
        in_specs=[
            pl.BlockSpec((None, tm, D), lambda b, i: (b, i, 0)),
            pl.BlockSpec((None, 8, D), lambda b, i: (b, jnp.maximum(i * r8 - 1, 0), 0)),
            pl.BlockSpec((None, 8, D), lambda b, i: (b, jnp.minimum((i + 1) * r8, nt8 - 1), 0)),
            _const_spec((1, D)),
            _const_spec((nc, D, 2 * FF_CHUNK)),
            _const_spec((nc, 3, 2 * FF_CHUNK)),
            _const_spec((nc, 1, 2 * FF_CHUNK)),
            _const_spec((nc, FF_CHUNK, D)),
            _const_spec((1, D)),
        ],
        out_specs=pl.BlockSpec((None, tm, D), lambda b, i: (b, i, 0)),
        out_shape=jax.ShapeDtypeStruct((B, S, D), F32),
        scratch_shapes=[
            pltpu.VMEM((tm + 16, 2 * FF_CHUNK), F32),
            pltpu.VMEM((tm, D), F32),
        ],
        compiler_params=_cparams(("parallel", "parallel")),
        name="conv_ffn",
    )(x, x, x, gpre, wup, cw, cb, wdn, gpost)


def _qkv_b_kernel(x_ref, g_ref, w_ref, ck_ref, sk_ref, q_ref, k_ref, v_ref):
    xn = _rms(x_ref[...], g_ref[...]).astype(BF16)
    ck = ck_ref[...]
    sk = sk_ref[...]
    nq = q_ref.shape[1]
    D = D_MODEL
    for c in range(w_ref.shape[1] // D):
        y = jnp.dot(xn, w_ref[:, c * D:(c + 1) * D], preferred_element_type=F32)
        if (c + 1) * D <= 2 * nq:
            dst, off, scale = (q_ref, c * D, Q_SCALE) if c * D < nq else (k_ref, c * D - nq, 1.0)
            for h in range(D // PAIR):
                yh = y[:, h * PAIR:(h + 1) * PAIR]
                r = yh * ck + pltpu.roll(yh, HEAD_DIM, 1) * sk
                if scale != 1.0:
                    r = r * scale
                dst[:, off + h * PAIR:off + (h + 1) * PAIR] = r.astype(BF16)
        else:
            v_ref[...] = y.astype(BF16)


def _qkv_b(x, g, w, ck, sk):
    B, S, D = x.shape
    tm = min(TM, S)
    G = len(DL_PATTERNS)
    nq = G * DL_HEADS * HEAD_DIM
    nv = DL_HEADS * HEAD_DIM
    return pl.pallas_call(
        _qkv_b_kernel,
        grid=(B, S // tm),
        in_specs=[
            pl.BlockSpec((None, tm, D), lambda b, i: (b, i, 0)),
            _const_spec((1, D)),
            _const_spec((D, 2 * nq + nv)),
            pl.BlockSpec((tm, PAIR), lambda b, i: (i, 0)),
            pl.BlockSpec((tm, PAIR), lambda b, i: (i, 0)),
        ],
        out_specs=[
            pl.BlockSpec((None, tm, nq), lambda b, i: (b, i, 0)),
            pl.BlockSpec((None, tm, nq), lambda b, i: (b, i, 0)),
            pl.BlockSpec((None, tm, nv), lambda b, i: (b, i, 0)),
        ],
        out_shape=[
            jax.ShapeDtypeStruct((B, S, nq), BF16),
            jax.ShapeDtypeStruct((B, S, nq), BF16),
            jax.ShapeDtypeStruct((B, S, nv), BF16),
        ],
        compiler_params=_cparams(("parallel", "parallel")),
        name="qkv_b",
    )(x, g, w, ck, sk)


def _dil_attn_kernel(q_ref, k_ref, kp_ref, kn_ref, v_ref, vp_ref, vn_ref, o_ref, e_ref, kx_ref, vx_ref):
    i = pl.program_id(2)
    last = pl.num_programs(2) - 1
    tq = q_ref.shape[0]
    H = DIL_HALF
    kx_ref[0:H, :] = kp_ref[...]
    kx_ref[H:H + tq, :] = k_ref[...]
    kx_ref[H + tq:, :] = kn_ref[...]
    vx_ref[0:H, :] = vp_ref[...]
    vx_ref[H:H + tq, :] = v_ref[...]
    vx_ref[H + tq:, :] = vn_ref[...]

    sub = DIL_SUB
    win = sub + 2 * H
    n_sub = tq // sub
    jq = lax.broadcasted_iota(jnp.int32, (sub, win), 0)
    ck = lax.broadcasted_iota(jnp.int32, (sub, win), 1)
    band = (ck >= jq) & (ck <= jq + 2 * H)
    band_first = band & (ck >= jnp.where(i > 0, 0, H))
    band_last = band & (ck < jnp.where(i < last, win, win - H))
    lane = lax.broadcasted_iota(jnp.int32, (sub, PAIR), 1)
    head_a = (lane % HEAD_DIM) < (HEAD_DIM // 2)
    out_a = lane < HEAD_DIM

    for s in range(n_sub):
        valid = band_first if s == 0 else (band_last if s == n_sub - 1 else band)
        if n_sub == 1:
            valid = band_first & band_last
        for hp in range(q_ref.shape[1] // PAIR):
            cols = slice(hp * PAIR, (hp + 1) * PAIR)
            qp = q_ref[s * sub:(s + 1) * sub, cols]
            kw = kx_ref[s * sub:s * sub + win, cols]
            vw = vx_ref[s * sub:s * sub + win, cols]
            zero = jnp.zeros_like(qp)
            res = []
            for qh in (jnp.where(head_a, qp, zero), jnp.where(head_a, zero, qp)):
                sc = lax.dot_general(qh, kw, (((1,), (1,)), ((), ())), preferred_element_type=F32)
                sc = jnp.where(valid, sc, NEG)
                m = jnp.max(sc, axis=-1, keepdims=True)
                p = jnp.exp2(sc - m)
                l = jnp.sum(p, axis=-1, keepdims=True)
                o = jnp.dot(p.astype(BF16), vw, preferred_element_type=F32) * (1.0 / l)
                res.append((o, m + jnp.log2(l)))
            o_ref[s * sub:(s + 1) * sub, cols] = jnp.where(out_a, res[0][0], res[1][0]).astype(o_ref.dtype)
            e_ref[s * sub:(s + 1) * sub, cols] = jnp.where(out_a, res[0][1], res[1][1])


def _dil_attn(q, k, v, g, dil):
    B, S, nq = q.shape
    G = len(DL_PATTERNS)
    C = DL_HEADS * HEAD_DIM
    L = S // dil
    tq = min(DIL_TQ, L)
    H = DIL_HALF
    nh = L // H
    rq = tq // H
    qv = q.reshape(B, L, dil * nq)
    kv = k.reshape(B, L, dil * nq)
    vv = v.reshape(B, L, dil * C)
    qk_main = lambda b, r, i: (b, i, r * G + g)
    qk_prev = lambda b, r, i: (b, jnp.maximum(i * rq - 1, 0), r * G + g)
    qk_next = lambda b, r, i: (b, jnp.minimum((i + 1) * rq, nh - 1), r * G + g)
    v_main = lambda b, r, i: (b, i, r)
    v_prev = lambda b, r, i: (b, jnp.maximum(i * rq - 1, 0), r)
    v_next = lambda b, r, i: (b, jnp.minimum((i + 1) * rq, nh - 1), r)
    o, e = pl.pallas_call(
        _dil_attn_kernel,
        grid=(B, dil, L // tq),
        in_specs=[
            pl.BlockSpec((None, tq, C), qk_main),
            pl.BlockSpec((None, tq, C), qk_main),
            pl.BlockSpec((None, H, C), qk_prev),
            pl.BlockSpec((None, H, C), qk_next),
            pl.BlockSpec((None, tq, C), v_main),
            pl.BlockSpec((None, H, C), v_prev),
            pl.BlockSpec((None, H, C), v_next),
        ],
        out_specs=[
            pl.BlockSpec((None, tq, C), v_main),
            pl.BlockSpec((None, tq, C), v_main),
        ],
        out_shape=[
            jax.ShapeDtypeStruct((B, L, dil * C), BF16),
            jax.ShapeDtypeStruct((B, L, dil * C), F32),
        ],
        scratch_shapes=[
            pltpu.VMEM((tq + 2 * H, C), BF16),
            pltpu.VMEM((tq + 2 * H, C), BF16),
        ],
        compiler_params=_cparams(("parallel", "parallel", "parallel")),
        name=f"dil_attn_{dil}",
    )(qv, kv, kv, kv, vv, vv, vv)
    return o.reshape(B, S, C), e.reshape(B, S, C)


def _rope_tables(S):
    inv = 1.0 / (ROPE_THETA ** (jnp.arange(0, HEAD_DIM, 2, dtype=F32) / HEAD_DIM))
    ang = jnp.arange(S, dtype=F32)[:, None] * inv[None, :]
    cos, sin = jnp.cos(ang), jnp.sin(ang)
    ck = jnp.concatenate([cos] * 4, axis=1)
    sk = jnp.concatenate([-sin, -sin, sin, sin], axis=1)
    ct = jnp.concatenate([cos.T, cos.T], axis=0)
    st = jnp.concatenate([sin.T, sin.T], axis=0)
    return ck, sk, ct, st


def _prep_ffn(w_up, conv_w, conv_b, w_down):
    nc = D_FF // FF_CHUNK
    D = w_up.shape[0]

    def chunk_cols(a):
        g = a[..., :D_FF].reshape(a.shape[:-1] + (nc, FF_CHUNK))
        u = a[..., D_FF:].reshape(a.shape[:-1] + (nc, FF_CHUNK))
        return jnp.moveaxis(jnp.concatenate([g, u], axis=-1), -2, 0)

    return (chunk_cols(w_up).astype(BF16), chunk_cols(conv_w), chunk_cols(conv_b[None, :]),
            w_down.reshape(nc, FF_CHUNK, D).astype(BF16))


def _trunk(x, p):
    B, S, D = x.shape
    ck, sk, ct, st = _rope_tables(S)
    row = lambda a: a.reshape(1, -1)
    for i in range(DEPTH):
        j = i // N_MIXERS
        if i % N_MIXERS == 0:
            lambda_init = 0.8 - 0.6 * math.exp(-0.3 * i)
            n = DA_HEADS * PAIR
            perm = _rope_perm(n)
            w_in = p['da_w_in'][j]
            wk = w_in[:, n:2 * n][:, perm].astype(BF16)
            wqv_t = jnp.concatenate([w_in[:, :n][:, perm], w_in[:, 2 * n:]], axis=1).T.astype(BF16)
            k, qt, vt = _qkv_a(x, row(p['norm_mix_pre'][i]), wk, wqv_t, ck, sk, ct, st)
            o = _diff_attn(qt, k, vt, row(p['da_lambda_q1'][j]), row(p['da_lambda_k1'][j]),
                           row(p['da_lambda_q2'][j]), row(p['da_lambda_k2'][j]),
                           row(p['da_subln'][j]), lambda_init)
            x = _out_proj(o, p['da_w_out'][j].astype(BF16), row(p['norm_mix_post'][i]), x)
        else:
            G = len(DL_PATTERNS)
            nq = G * DL_HEADS * HEAD_DIM
            perm = _rope_perm(nq)
            w_in = p['dl_w_in'][j]
            w = jnp.concatenate([w_in[:, :nq][:, perm], w_in[:, nq:2 * nq][:, perm], w_in[:, 2 * nq:]],
                                axis=1).astype(BF16)
            q, k, v = _qkv_b(x, row(p['norm_mix_pre'][i]), w, ck, sk)
            os, es = [], []
            for g, (window, dil) in enumerate(DL_PATTERNS):
                assert window // (2 * dil) == DIL_HALF
                o_g, e_g = _dil_attn(q, k, v, g, dil)
                os.append(o_g)
                es.append(e_g)
            x = _combine_out_proj(os, es, p['dl_w_out'][j].astype(BF16), row(p['norm_mix_post'][i]), x)
        wup, cw, cb, wdn = _prep_ffn(p['ffn_w_up'][i], p['ffn_conv_w'][i], p['ffn_conv_b'][i], p['ffn_w_down'][i])
        x = _ffn(x, row(p['norm_ffn_pre'][i]), wup, cw, cb, wdn, row(p['norm_ffn_post'][i]))
    return x


def kernel(x_prompt, x_sample, norm_mix_pre, norm_mix_post, norm_ffn_pre, norm_ffn_post, da_w_in, da_w_out, da_lambda_q1, da_lambda_k1, da_lambda_q2, da_lambda_k2, da_subln, dl_w_in, dl_w_out, ffn_w_up, ffn_conv_w, ffn_conv_b, ffn_w_down):
    p = dict(norm_mix_pre=norm_mix_pre, norm_mix_post=norm_mix_post, norm_ffn_pre=norm_ffn_pre,
             norm_ffn_post=norm_ffn_post, da_w_in=da_w_in, da_w_out=da_w_out, da_lambda_q1=da_lambda_q1,
             da_lambda_k1=da_lambda_k1, da_lambda_q2=da_lambda_q2, da_lambda_k2=da_lambda_k2,
             da_subln=da_subln, dl_w_in=dl_w_in, dl_w_out=dl_w_out, ffn_w_up=ffn_w_up,
             ffn_conv_w=ffn_conv_w, ffn_conv_b=ffn_conv_b, ffn_w_down=ffn_w_down)
    return (_trunk(x_prompt, p), _trunk(x_sample, p))
```

```python
import functools
import math

import numpy as np
import jax
import jax.numpy as jnp
from jax import lax
from jax.experimental import pallas as pl
from jax.experimental.pallas import tpu as pltpu

D_MODEL = 1024
DEPTH = 2
N_MIXERS = 2
DA_HEADS = 8
HEAD_DIM = 64
DL_HEADS = 16
DL_PATTERNS = ((128, 1), (512, 4), (2048, 16))
D_FF = 2816
ROPE_THETA = 10000.0
NORM_EPS = 1e-6
NEG = -1e30

LANES = 128
PAIR = 2 * HEAD_DIM
VMEM_LIMIT = 56 * 1024 * 1024
Q_SCALE = (HEAD_DIM ** -0.5) * math.log2(math.e)

TM = 512
TQ = 512
TK = 512
KV_UNROLL = 4
FF_UNROLL = 2
FF_CHUNK = 256
DIL_TQ = 1024
DIL_SUB = 128
DIL_HALF = 64
VT_ROWS = PAIR + 16

BF16 = jnp.bfloat16
F32 = jnp.float32


def _cparams(sem):
    return pltpu.CompilerParams(dimension_semantics=sem, vmem_limit_bytes=VMEM_LIMIT)


def _const_spec(shape):
    nd = len(shape)
    return pl.BlockSpec(shape, lambda *_: (0,) * nd, pipeline_mode=pl.Buffered(1))


def _rms(x, g):
    ms = jnp.mean(x * x, axis=-1, keepdims=True)
    return x * lax.rsqrt(ms + NORM_EPS) * g


def _rope_perm(n):
    return np.arange(n).reshape(n // PAIR, 2, 2, HEAD_DIM // 2).transpose(0, 2, 1, 3).reshape(-1)


def _qkv_a_kernel(x_ref, g_ref, wk_ref, wqv_ref, ck_ref, sk_ref, ct_ref, st_ref,
                  k_ref, qt_ref, vt_ref):
    xn = _rms(x_ref[...], g_ref[...]).astype(BF16)
    k = jnp.dot(xn, wk_ref[...], preferred_element_type=F32)
    ck = ck_ref[...]
    sk = sk_ref[...]
    for h in range(DA_HEADS):
        kh = k[:, h * PAIR:(h + 1) * PAIR]
        k_ref[:, h * PAIR:(h + 1) * PAIR] = (kh * ck + pltpu.roll(kh, HEAD_DIM, 1) * sk).astype(BF16)
    qvt = lax.dot_general(wqv_ref[...], xn, (((1,), (1,)), ((), ())), preferred_element_type=F32)
    ct = ct_ref[...]
    st = st_ref[...]
    for h in range(DA_HEADS):
        x1 = qvt[h * PAIR:h * PAIR + HEAD_DIM]
        x2 = qvt[h * PAIR + HEAD_DIM:(h + 1) * PAIR]
        qt_ref[h * PAIR:h * PAIR + HEAD_DIM, :] = ((x1 * ct - x2 * st) * Q_SCALE).astype(BF16)
        qt_ref[h * PAIR + HEAD_DIM:(h + 1) * PAIR, :] = ((x1 * st + x2 * ct) * Q_SCALE).astype(BF16)
    ones = jnp.ones((VT_ROWS - PAIR, vt_ref.shape[1]), BF16)
    for h in range(DA_HEADS):
        vt_ref[h * VT_ROWS:h * VT_ROWS + PAIR, :] = qvt[D_MODEL + h * PAIR:D_MODEL + (h + 1) * PAIR].astype(BF16)
        vt_ref[h * VT_ROWS + PAIR:(h + 1) * VT_ROWS, :] = ones


def _qkv_a(x, g, wk, wqv_t, ck, sk, ct, st):
    B, S, D = x.shape
    tm = min(TM, S)
    nqk = DA_HEADS * PAIR
    nvt = DA_HEADS * VT_ROWS
    return pl.pallas_call(
        _qkv_a_kernel,
        grid=(B, S // tm),
        in_specs=[
            pl.BlockSpec((None, tm, D), lambda b, i: (b, i, 0)),
            _const_spec((1, D)),
            _const_spec((D, nqk)),
            _const_spec((2 * nqk, D)),
            pl.BlockSpec((tm, PAIR), lambda b, i: (i, 0)),
            pl.BlockSpec((tm, PAIR), lambda b, i: (i, 0)),
            pl.BlockSpec((HEAD_DIM, tm), lambda b, i: (0, i)),
            pl.BlockSpec((HEAD_DIM, tm), lambda b, i: (0, i)),
        ],
        out_specs=[
            pl.BlockSpec((None, tm, nqk), lambda b, i: (b, i, 0)),
            pl.BlockSpec((None, nqk, tm), lambda b, i: (b, 0, i)),
            pl.BlockSpec((None, nvt, tm), lambda b, i: (b, 0, i)),
        ],
        out_shape=[
            jax.ShapeDtypeStruct((B, S, nqk), BF16),
            jax.ShapeDtypeStruct((B, nqk, S), BF16),
            jax.ShapeDtypeStruct((B, nvt, S), BF16),
        ],
        compiler_params=_cparams(("parallel", "parallel")),
        name="qkv_a",
    )(x, g, wk, wqv_t, ck, sk, ct, st)


def _diff_attn_kernel(qt_ref, k_ref, vt_ref, lq1_ref, lk1_ref, lq2_ref, lk2_ref, g_ref, o_ref,
                      sa_ref, sb_ref, m_ref, acc_ref, *, tk, lambda_init):
    S = k_ref.shape[0]
    n = S // tk
    qt = qt_ref[...]
    row = lax.broadcasted_iota(jnp.int32, qt.shape, 0)
    comp0 = (row % HEAD_DIM) < (HEAD_DIM // 2)
    zero = jnp.zeros_like(qt)
    qts = (jnp.where(comp0, qt, zero), jnp.where(comp0, zero, qt))

    m_ref[...] = jnp.full(m_ref.shape, NEG, F32)
    acc_ref[...] = jnp.zeros(acc_ref.shape, F32)

    def scores(j, s_ref):
        kb = k_ref[pl.ds(pl.multiple_of(j * tk, tk), tk), :]
        for c in range(2):
            s_ref[c] = jnp.dot(kb, qts[c], preferred_element_type=F32)

    def softmax_pv(j, s_ref):
        vtb = vt_ref[:, pl.ds(pl.multiple_of(j * tk, tk), tk)]
        for c in range(2):
            s = s_ref[c]
            m_old = m_ref[c:c + 1, :]
            m_new = jnp.maximum(m_old, jnp.max(s, axis=0, keepdims=True))
            alpha = jnp.exp2(m_old - m_new)
            p = jnp.exp2(s - m_new).astype(BF16)
            acc_ref[c] = alpha * acc_ref[c] + jnp.dot(vtb, p, preferred_element_type=F32)
            m_ref[c:c + 1, :] = m_new

    bufs = (sa_ref, sb_ref)
    scores(0, sa_ref)

    def body(jj, carry):
        j = KV_UNROLL * jj
        for u in range(KV_UNROLL):
            scores(j + u + 1, bufs[(u + 1) % 2])
            softmax_pv(j + u, bufs[u % 2])
        return carry

    trips = (n - 1) // KV_UNROLL
    lax.fori_loop(0, trips, body, 0)
    done = trips * KV_UNROLL
    for u in range(n - done):
        if done + u + 1 < n:
            scores(done + u + 1, bufs[(u + 1) % 2])
        softmax_pv(done + u, bufs[u % 2])

    lam = (jnp.exp(jnp.sum(lq1_ref[...] * lk1_ref[...], axis=-1, keepdims=True))
           - jnp.exp(jnp.sum(lq2_ref[...] * lk2_ref[...], axis=-1, keepdims=True)) + lambda_init)
    o0 = acc_ref[0, :PAIR, :] * (1.0 / acc_ref[0, PAIR:PAIR + 1, :])
    o1 = acc_ref[1, :PAIR, :] * (1.0 / acc_ref[1, PAIR:PAIR + 1, :])
    ot = o0 - lam * o1
    ms = jnp.mean(ot * ot, axis=0, keepdims=True)
    on = (ot * lax.rsqrt(ms + NORM_EPS)).T
    o_ref[...] = (on * g_ref[...] * (1.0 - lambda_init)).astype(o_ref.dtype)


def _diff_attn(qt, k, vt, lq1, lk1, lq2, lk2, subln_g, lambda_init):
    B, nqk, S = qt.shape
    tq = min(TQ, S)
    tk = min(TK, S)
    lam_spec = _const_spec((1, HEAD_DIM))
    return pl.pallas_call(
        functools.partial(_diff_attn_kernel, tk=tk, lambda_init=lambda_init),
        grid=(B, DA_HEADS, S // tq),
        in_specs=[
            pl.BlockSpec((None, PAIR, tq), lambda b, h, i: (b, h, i)),
            pl.BlockSpec((None, S, PAIR), lambda b, h, i: (b, 0, h)),
            pl.BlockSpec((None, VT_ROWS, S), lambda b, h, i: (b, h, 0)),
            lam_spec, lam_spec, lam_spec, lam_spec,
            _const_spec((1, PAIR)),
        ],
        out_specs=pl.BlockSpec((None, tq, PAIR), lambda b, h, i: (b, i, h)),
        out_shape=jax.ShapeDtypeStruct((B, S, nqk), BF16),
        scratch_shapes=[
            pltpu.VMEM((2, tk, tq), F32),
            pltpu.VMEM((2, tk, tq), F32),
            pltpu.VMEM((8, tq), F32),
            pltpu.VMEM((2, VT_ROWS, tq), F32),
        ],
        compiler_params=_cparams(("parallel", "parallel", "arbitrary")),
        name="diff_attn",
    )(qt, k, vt, lq1, lk1, lq2, lk2, subln_g)


def _out_proj_kernel(o_ref, w_ref, g_ref, x_ref, y_ref):
    h = jnp.dot(o_ref[...], w_ref[...], preferred_element_type=F32)
    y_ref[...] = x_ref[...] + _rms(h, g_ref[...])


def _out_proj(o, w, g, x):
    B, S, D = x.shape
    tm = min(TM, S)
    tok = pl.BlockSpec((None, tm, D), lambda b, i: (b, i, 0))
    return pl.pallas_call(
        _out_proj_kernel,
        grid=(B, S // tm),
        in_specs=[tok, _const_spec((D, D)), _const_spec((1, D)), tok],
        out_specs=tok,
        out_shape=jax.ShapeDtypeStruct((B, S, D), F32),
        compiler_params=_cparams(("parallel", "parallel")),
        name="out_proj",
    )(o, w, g, x)


def _combine_out_proj_kernel(*refs):
    G = len(DL_PATTERNS)
    o_refs, e_refs = refs[:G], refs[G:2 * G]
    w_ref, g_ref, x_ref, y_ref = refs[2 * G:2 * G + 4]
    ocat_ref = refs[2 * G + 4]
    staged = [g for g, (_, dil) in enumerate(DL_PATTERNS) if dil > 1]
    buf_refs = {}
    for n, g in enumerate(staged):
        buf_refs[g] = refs[2 * G + 5 + n]
        buf_refs[G + g] = refs[2 * G + 5 + len(staged) + n]
    tm, C = x_ref.shape
    n_slab = C // LANES

    for g, (_, dil) in enumerate(DL_PATTERNS):
        if dil == 1:
            continue
        rows = tm // dil
        for ref, buf_ref in ((o_refs[g], buf_refs[g]), (e_refs[g], buf_refs[G + g])):
            for r in range(dil):
                for s in range(n_slab):
                    buf_ref[s, pl.ds(r, rows, stride=dil), :] = (
                        ref[:, r * C + s * LANES:r * C + (s + 1) * LANES].astype(F32))

    def slab(g, s, kind):
        if DL_PATTERNS[g][1] == 1:
            ref = (o_refs, e_refs)[kind][g]
            return ref[:, s * LANES:(s + 1) * LANES].astype(F32)
        return buf_refs[kind * G + g][s]

    for s in range(n_slab):
        es = [slab(g, s, 1) for g in range(G)]
        m = functools.reduce(jnp.maximum, es)
        ws = [jnp.exp2(e - m) for e in es]
        o = sum(w * slab(g, s, 0) for g, w in enumerate(ws)) * (1.0 / sum(ws))
        ocat_ref[:, s * LANES:(s + 1) * LANES] = o.astype(BF16)
    h = jnp.dot(ocat_ref[...], w_ref[...], preferred_element_type=F32)
    y_ref[...] = x_ref[...] + _rms(h, g_ref[...])


def _combine_out_proj(os, es, w, g, x):
    B, S, D = x.shape
    tm = min(TM, S)
    tok = pl.BlockSpec((None, tm, D), lambda b, i: (b, i, 0))
    views = [pl.BlockSpec((None, tm // dil, dil * D), lambda b, i: (b, i, 0)) for _, dil in DL_PATTERNS]
    return pl.pallas_call(
        _combine_out_proj_kernel,
        grid=(B, S // tm),
        in_specs=views + views + [_const_spec((D, D)), _const_spec((1, D)), tok],
        out_specs=tok,
        out_shape=jax.ShapeDtypeStruct((B, S, D), F32),
        scratch_shapes=[pltpu.VMEM((tm, D), BF16)]
        + [pltpu.VMEM((D // LANES, tm, LANES), F32)] * (2 * sum(dil > 1 for _, dil in DL_PATTERNS)),
        compiler_params=_cparams(("parallel", "parallel")),
        name="combine_out_proj",
    )(*os, *es, w, g, x)


def _ffn_kernel(x_ref, xp_ref, xn_ref, gpre_ref, wup_ref, cw_ref, cb_ref, wdn_ref, gpost_ref, y_ref,
                ha_ref, hb_ref, acc_ref):
    i = pl.program_id(1)
    last = pl.num_programs(1) - 1
    tm = x_ref.shape[0]
    x = x_ref[...]
    gpre = gpre_ref[...]
    keep_prev = (i > 0).astype(F32)
    keep_next = (i < last).astype(F32)
    xe = jnp.concatenate([_rms(xp_ref[...], gpre) * keep_prev, _rms(x, gpre),
                          _rms(xn_ref[...], gpre) * keep_next], axis=0).astype(BF16)
    acc_ref[...] = jnp.zeros(acc_ref.shape, F32)
    n_chunks = wup_ref.shape[0]

    def up_proj(c, h_ref):
        h_ref[...] = jnp.dot(xe, wup_ref[c], preferred_element_type=F32)

    def conv_act_down(c, h_ref):
        cw = cw_ref[c]
        y = (h_ref[pl.ds(7, tm), :] * cw[0:1] + h_ref[pl.ds(8, tm), :] * cw[1:2]
             + h_ref[pl.ds(9, tm), :] * cw[2:3] + cb_ref[c])
        gate = y[:, :FF_CHUNK]
        up = y[:, FF_CHUNK:]
        cdf = 0.5 * (1.0 + jnp.tanh(math.sqrt(2.0 / math.pi) * (gate + 0.044715 * (gate * gate * gate))))
        act = (gate * cdf * up).astype(BF16)
        acc_ref[...] += jnp.dot(act, wdn_ref[c], preferred_element_type=F32)

    bufs = (ha_ref, hb_ref)
    up_proj(0, ha_ref)

    def body(cc, carry):
        c = FF_UNROLL * cc
        for u in range(FF_UNROLL):
            up_proj(c + u + 1, bufs[(u + 1) % 2])
            conv_act_down(c + u, bufs[u % 2])
        return carry

    trips = (n_chunks - 1) // FF_UNROLL
    lax.fori_loop(0, trips, body, 0)
    done = trips * FF_UNROLL
    for u in range(n_chunks - done):
        if done + u + 1 < n_chunks:
            up_proj(done + u + 1, bufs[(u + 1) % 2])
        conv_act_down(done + u, bufs[u % 2])
    y_ref[...] = x + _rms(acc_ref[...], gpost_ref[...])


def _ffn(x, gpre, wup, cw, cb, wdn, gpost):
    B, S, D = x.shape
    tm = min(TM, S)
    nt8 = S // 8
    r8 = tm // 8
    nc = wup.shape[0]
    return pl.pallas_call(
        _ffn_kernel,
        grid=(B, S // tm),
        in_specs=[
            pl.BlockSpec((None, tm, D), lambda b, i: (b, i, 0)),
            pl.BlockSpec((None, 8, D), lambda b, i: (b, jnp.maximum(i * r8 - 1, 0), 0)),
            pl.BlockSpec((None, 8, D), lambda b, i: (b, jnp.minimum((i + 1) * r8, nt8 - 1), 0)),
            _const_spec((1, D)),
            _const_spec((nc, D, 2 * FF_CHUNK)),
            _const_spec((nc, 3, 2 * FF_CHUNK)),
            _const_spec((nc, 1, 2 * FF_CHUNK)),
            _const_spec((nc, FF_CHUNK, D)),
            _const_spec((1, D)),
        ],
        out_specs=pl.BlockSpec((None, tm, D), lambda b, i: (b, i, 0)),
        out_shape=jax.ShapeDtypeStruct((B, S, D), F32),
        scratch_shapes=[
            pltpu.VMEM((tm + 16, 2 * FF_CHUNK), F32),
            pltpu.VMEM((tm + 16, 2 * FF_CHUNK), F32),
            pltpu.VMEM((tm, D), F32),
        ],
        compiler_params=_cparams(("parallel", "parallel")),
        name="conv_ffn",
    )(x, x, x, gpre, wup, cw, cb, wdn, gpost)


def _qkv_b_kernel(x_ref, g_ref, w_ref, ck_ref, sk_ref, *refs):
    G = len(DL_PATTERNS)
    q_refs, k_refs, v_refs = refs[:G], refs[G:2 * G], refs[2 * G:3 * G]
    ybuf_ref = refs[3 * G]
    tm = x_ref.shape[0]
    C = D_MODEL
    n_slab = C // LANES
    xn = _rms(x_ref[...], g_ref[...]).astype(BF16)
    ck = ck_ref[...]
    sk = sk_ref[...]

    def emit(slabs, dsts):
        if any(dil > 1 for _, dil in dsts):
            for s in range(n_slab):
                ybuf_ref[s] = slabs[s]
        for ref, dil in dsts:
            if dil == 1:
                for s in range(n_slab):
                    ref[:, s * LANES:(s + 1) * LANES] = slabs[s].astype(BF16)
            else:
                rows = tm // dil
                for r in range(dil):
                    for s in range(n_slab):
                        ref[:, r * C + s * LANES:r * C + (s + 1) * LANES] = (
                            ybuf_ref[s, pl.ds(r, rows, stride=dil), :].astype(BF16))

    for c in range(2 * G + 1):
        y = jnp.dot(xn, w_ref[:, c * C:(c + 1) * C], preferred_element_type=F32)
        slabs = [y[:, s * LANES:(s + 1) * LANES] for s in range(n_slab)]
        if c < 2 * G:
            scale = Q_SCALE if c < G else 1.0
            slabs = [yh * ck + pltpu.roll(yh, HEAD_DIM, 1) * sk for yh in slabs]
            if scale != 1.0:
                slabs = [yh * scale for yh in slabs]
            g = c % G
            emit(slabs, [((q_refs if c < G else k_refs)[g], DL_PATTERNS[g][1])])
        else:
            emit(slabs, [(v_refs[g], DL_PATTERNS[g][1]) for g in range(G)])


def _qkv_b(x, g, w, ck, sk):
    B, S, D = x.shape
    tm = min(TM, S)
    G = len(DL_PATTERNS)
    C = DL_HEADS * HEAD_DIM
    out_specs, out_shape = [], []
    for _ in range(3):
        for _, dil in DL_PATTERNS:
            out_specs.append(pl.BlockSpec((None, tm // dil, dil * C), lambda b, i: (b, i, 0)))
            out_shape.append(jax.ShapeDtypeStruct((B, S // dil, dil * C), BF16))
    return pl.pallas_call(
        _qkv_b_kernel,
        grid=(B, S // tm),
        in_specs=[
            pl.BlockSpec((None, tm, D), lambda b, i: (b, i, 0)),
            _const_spec((1, D)),
            _const_spec((D, (2 * G + 1) * C)),
            pl.BlockSpec((tm, PAIR), lambda b, i: (i, 0)),
            pl.BlockSpec((tm, PAIR), lambda b, i: (i, 0)),
        ],
        out_specs=out_specs,
        out_shape=out_shape,
        scratch_shapes=[pltpu.VMEM((C // LANES, tm, LANES), F32)],
        compiler_params=_cparams(("parallel", "parallel")),
        name="qkv_b",
    )(x, g, w, ck, sk)


def _dil_attn_kernel(q_ref, k_ref, kp_ref, kn_ref, v_ref, vp_ref, vn_ref, o_ref, e_ref, kx_ref, vx_ref):
    i = pl.program_id(2)
    last = pl.num_programs(2) - 1
    tq = q_ref.shape[0]
    H = DIL_HALF
    kx_ref[0:H, :] = kp_ref[...]
    kx_ref[H:H + tq, :] = k_ref[...]
    kx_ref[H + tq:, :] = kn_ref[...]
    vx_ref[0:H, :] = vp_ref[...]
    vx_ref[H:H + tq, :] = v_ref[...]
    vx_ref[H + tq:, :] = vn_ref[...]

    sub = DIL_SUB
    win = sub + 2 * H
    n_sub = tq // sub
    jq = lax.broadcasted_iota(jnp.int32, (sub, win), 0)
    ck = lax.broadcasted_iota(jnp.int32, (sub, win), 1)
    band = (ck >= jq) & (ck <= jq + 2 * H)
    band_first = band & (ck >= jnp.where(i > 0, 0, H))
    band_last = band & (ck < jnp.where(i < last, win, win - H))
    lane = lax.broadcasted_iota(jnp.int32, (sub, PAIR), 1)
    head_a = (lane % HEAD_DIM) < (HEAD_DIM // 2)
    out_a = lane < HEAD_DIM

    for s in range(n_sub):
        valid = band_first if s == 0 else (band_last if s == n_sub - 1 else band)
        if n_sub == 1:
            valid = band_first & band_last
        for hp in range(q_ref.shape[1] // PAIR):
            cols = slice(hp * PAIR, (hp + 1) * PAIR)
            qp = q_ref[s * sub:(s + 1) * sub, cols]
            kw = kx_ref[s * sub:s * sub + win, cols]
            vw = vx_ref[s * sub:s * sub + win, cols]
            zero = jnp.zeros_like(qp)
            res = []
            for qh in (jnp.where(head_a, qp, zero), jnp.where(head_a, zero, qp)):
                sc = lax.dot_general(qh, kw, (((1,), (1,)), ((), ())), preferred_element_type=F32)
                sc = jnp.where(valid, sc, NEG)
                m = jnp.max(sc, axis=-1, keepdims=True)
                p = jnp.exp2(sc - m)
                l = jnp.sum(p, axis=-1, keepdims=True)
                o = jnp.dot(p.astype(BF16), vw, preferred_element_type=F32) * (1.0 / l)
                res.append((o, m + jnp.log2(l)))
            o_ref[s * sub:(s + 1) * sub, cols] = jnp.where(out_a, res[0][0], res[1][0]).astype(o_ref.dtype)
            e_ref[s * sub:(s + 1) * sub, cols] = jnp.where(out_a, res[0][1], res[1][1])


def _dil_attn(q, k, v, dil):
    B, L, w = q.shape
    C = DL_HEADS * HEAD_DIM
    assert w == dil * C
    tq = min(DIL_TQ, L)
    H = DIL_HALF
    nh = L // H
    rq = tq // H
    main = lambda b, r, i: (b, i, r)
    prev = lambda b, r, i: (b, jnp.maximum(i * rq - 1, 0), r)
    nxt = lambda b, r, i: (b, jnp.minimum((i + 1) * rq, nh - 1), r)
    return pl.pallas_call(
        _dil_attn_kernel,
        grid=(B, dil, L // tq),
        in_specs=[
            pl.BlockSpec((None, tq, C), main),
            pl.BlockSpec((None, tq, C), main),
            pl.BlockSpec((None, H, C), prev),
            pl.BlockSpec((None, H, C), nxt),
            pl.BlockSpec((None, tq, C), main),
            pl.BlockSpec((None, H, C), prev),
            pl.BlockSpec((None, H, C), nxt),
        ],
        out_specs=[
            pl.BlockSpec((None, tq, C), main),
            pl.BlockSpec((None, tq, C), main),
        ],
        out_shape=[
            jax.ShapeDtypeStruct((B, L, dil * C), BF16),
            jax.ShapeDtypeStruct((B, L, dil * C), F32),
        ],
        scratch_shapes=[
            pltpu.VMEM((tq + 2 * H, C), BF16),
            pltpu.VMEM((tq + 2 * H, C), BF16),
        ],
        compiler_params=_cparams(("parallel", "parallel", "parallel")),
        name=f"dil_attn_{dil}",
    )(q, k, k, k, v, v, v)


def _rope_tables(S):
    inv = 1.0 / (ROPE_THETA ** (jnp.arange(0, HEAD_DIM, 2, dtype=F32) / HEAD_DIM))
    ang = jnp.arange(S, dtype=F32)[:, None] * inv[None, :]
    cos, sin = jnp.cos(ang), jnp.sin(ang)
    ck = jnp.concatenate([cos] * 4, axis=1)
    sk = jnp.concatenate([-sin, -sin, sin, sin], axis=1)
    ct = jnp.concatenate([cos.T, cos.T], axis=0)
    st = jnp.concatenate([sin.T, sin.T], axis=0)
    return ck, sk, ct, st


def _prep_ffn(w_up, conv_w, conv_b, w_down):
    nc = D_FF // FF_CHUNK
    D = w_up.shape[0]

    def chunk_cols(a):
        g = a[..., :D_FF].reshape(a.shape[:-1] + (nc, FF_CHUNK))
        u = a[..., D_FF:].reshape(a.shape[:-1] + (nc, FF_CHUNK))
        return jnp.moveaxis(jnp.concatenate([g, u], axis=-1), -2, 0)

    return (chunk_cols(w_up).astype(BF16), chunk_cols(conv_w), chunk_cols(conv_b[None, :]),
            w_down.reshape(nc, FF_CHUNK, D).astype(BF16))


def _trunk(x, p):
    B, S, D = x.shape
    ck, sk, ct, st = _rope_tables(S)
    row = lambda a: a.reshape(1, -1)
    for i in range(DEPTH):
        j = i // N_MIXERS
        if i % N_MIXERS == 0:
            lambda_init = 0.8 - 0.6 * math.exp(-0.3 * i)
            n = DA_HEADS * PAIR
            perm = _rope_perm(n)
            w_in = p['da_w_in'][j]
            wk = w_in[:, n:2 * n][:, perm].astype(BF16)
            wqv_t = jnp.concatenate([w_in[:, :n][:, perm], w_in[:, 2 * n:]], axis=1).T.astype(BF16)
            k, qt, vt = _qkv_a(x, row(p['norm_mix_pre'][i]), wk, wqv_t, ck, sk, ct, st)
            o = _diff_attn(qt, k, vt, row(p['da_lambda_q1'][j]), row(p['da_lambda_k1'][j]),
                           row(p['da_lambda_q2'][j]), row(p['da_lambda_k2'][j]),
                           row(p['da_subln'][j]), lambda_init)
            x = _out_proj(o, p['da_w_out'][j].astype(BF16), row(p['norm_mix_post'][i]), x)
        else:
            G = len(DL_PATTERNS)
            nq = G * DL_HEADS * HEAD_DIM
            perm = _rope_perm(nq)
            w_in = p['dl_w_in'][j]
            w = jnp.concatenate([w_in[:, :nq][:, perm], w_in[:, nq:2 * nq][:, perm], w_in[:, 2 * nq:]],
                                axis=1).astype(BF16)
            qkv = _qkv_b(x, row(p['norm_mix_pre'][i]), w, ck, sk)
            os, es = [], []
            for g, (window, dil) in enumerate(DL_PATTERNS):
                assert window // (2 * dil) == DIL_HALF
                o_g, e_g = _dil_attn(qkv[g], qkv[G + g], qkv[2 * G + g], dil)
                os.append(o_g)
                es.append(e_g)
            x = _combine_out_proj(os, es, p['dl_w_out'][j].astype(BF16), row(p['norm_mix_post'][i]), x)
        wup, cw, cb, wdn = _prep_ffn(p['ffn_w_up'][i], p['ffn_conv_w'][i], p['ffn_conv_b'][i], p['ffn_w_down'][i])
        x = _ffn(x, row(p['norm_ffn_pre'][i]), wup, cw, cb, wdn, row(p['norm_ffn_post'][i]))
    return x


def kernel(x_prompt, x_sample, norm_mix_pre, norm_mix_post, norm_ffn_pre, norm_ffn_post, da_w_in, da_w_out, da_lambda_q1, da_lambda_k1, da_lambda_q2, da_lambda_k2, da_subln, dl_w_in, dl_w_out, ffn_w_up, ffn_conv_w, ffn_conv_b, ffn_w_down):
    p = dict(norm_mix_pre=norm_mix_pre, norm_mix_post=norm_mix_post, norm_ffn_pre=norm_ffn_pre,
             norm_ffn_post=norm_ffn_post, da_w_in=da_w_in, da_w_out=da_w_out, da_lambda_q1=da_lambda_q1,
             da_lambda_k1=da_lambda_k1, da_lambda_q2=da_lambda_q2, da_lambda_k2=da_lambda_k2,
             da_subln=da_subln, dl_w_in=dl_w_in, dl_w_out=dl_w_out, ffn_w_up=ffn_w_up,
             ffn_conv_w=ffn_conv_w, ffn_conv_b=ffn_conv_b, ffn_w_down=ffn_w_down)
    return (_trunk(x_prompt, p), _trunk(x_sample, p))
```

```python
import functools
import math

import numpy as np
import jax
import jax.numpy as jnp
from jax import lax
from jax.experimental import pallas as pl
from jax.experimental.pallas import tpu as pltpu

D_MODEL = 1024
DEPTH = 2
N_MIXERS = 2
DA_HEADS = 8
HEAD_DIM = 64
DL_HEADS = 16
DL_PATTERNS = ((128, 1), (512, 4), (2048, 16))
D_FF = 2816
ROPE_THETA = 10000.0
NORM_EPS = 1e-6
NEG = -1e30

LANES = 128
PAIR = 2 * HEAD_DIM
VMEM_LIMIT = 56 * 1024 * 1024
Q_SCALE = (HEAD_DIM ** -0.5) * math.log2(math.e)

TM = 512
TQ = 512
TK = 512
KV_UNROLL = 8
FF_UNROLL = 2
FF_CHUNK = 256
DIL_TQ = 1024
DIL_SUB = 128
DIL_HALF = 64
VT_ROWS = PAIR + 16

BF16 = jnp.bfloat16
F32 = jnp.float32


def _cparams(sem):
    return pltpu.CompilerParams(dimension_semantics=sem, vmem_limit_bytes=VMEM_LIMIT)


def _const_spec(shape):
    nd = len(shape)
    return pl.BlockSpec(shape, lambda *_: (0,) * nd, pipeline_mode=pl.Buffered(1))


def _rms(x, g):
    ms = jnp.mean(x * x, axis=-1, keepdims=True)
    return x * lax.rsqrt(ms + NORM_EPS) * g


def _rope_perm(n):
    return np.arange(n).reshape(n // PAIR, 2, 2, HEAD_DIM // 2).transpose(0, 2, 1, 3).reshape(-1)


def _qkv_a_kernel(x_ref, g_ref, wk_ref, wqv_ref, ck_ref, sk_ref, ct_ref, st_ref,
                  k_ref, qt_ref, vt_ref):
    xn = _rms(x_ref[...], g_ref[...]).astype(BF16)
    k = jnp.dot(xn, wk_ref[...], preferred_element_type=F32)
    ck = ck_ref[...]
    sk = sk_ref[...]
    for h in range(DA_HEADS):
        kh = k[:, h * PAIR:(h + 1) * PAIR]
        k_ref[:, h * PAIR:(h + 1) * PAIR] = (kh * ck + pltpu.roll(kh, HEAD_DIM, 1) * sk).astype(BF16)
    qvt = lax.dot_general(wqv_ref[...], xn, (((1,), (1,)), ((), ())), preferred_element_type=F32)
    ct = ct_ref[...]
    st = st_ref[...]
    for h in range(DA_HEADS):
        x1 = qvt[h * PAIR:h * PAIR + HEAD_DIM]
        x2 = qvt[h * PAIR + HEAD_DIM:(h + 1) * PAIR]
        qt_ref[h * PAIR:h * PAIR + HEAD_DIM, :] = ((x1 * ct - x2 * st) * Q_SCALE).astype(BF16)
        qt_ref[h * PAIR + HEAD_DIM:(h + 1) * PAIR, :] = ((x1 * st + x2 * ct) * Q_SCALE).astype(BF16)
    ones = jnp.ones((VT_ROWS - PAIR, vt_ref.shape[1]), BF16)
    for h in range(DA_HEADS):
        vt_ref[h * VT_ROWS:h * VT_ROWS + PAIR, :] = qvt[D_MODEL + h * PAIR:D_MODEL + (h + 1) * PAIR].astype(BF16)
        vt_ref[h * VT_ROWS + PAIR:(h + 1) * VT_ROWS, :] = ones


def _qkv_a(x, g, wk, wqv_t, ck, sk, ct, st):
    B, S, D = x.shape
    tm = min(TM, S)
    nqk = DA_HEADS * PAIR
    nvt = DA_HEADS * VT_ROWS
    return pl.pallas_call(
        _qkv_a_kernel,
        grid=(B, S // tm),
        in_specs=[
            pl.BlockSpec((None, tm, D), lambda b, i: (b, i, 0)),
            _const_spec((1, D)),
            _const_spec((D, nqk)),
            _const_spec((2 * nqk, D)),
            pl.BlockSpec((tm, PAIR), lambda b, i: (i, 0)),
            pl.BlockSpec((tm, PAIR), lambda b, i: (i, 0)),
            pl.BlockSpec((HEAD_DIM, tm), lambda b, i: (0, i)),
            pl.BlockSpec((HEAD_DIM, tm), lambda b, i: (0, i)),
        ],
        out_specs=[
            pl.BlockSpec((None, tm, nqk), lambda b, i: (b, i, 0)),
            pl.BlockSpec((None, nqk, tm), lambda b, i: (b, 0, i)),
            pl.BlockSpec((None, nvt, tm), lambda b, i: (b, 0, i)),
        ],
        out_shape=[
            jax.ShapeDtypeStruct((B, S, nqk), BF16),
            jax.ShapeDtypeStruct((B, nqk, S), BF16),
            jax.ShapeDtypeStruct((B, nvt, S), BF16),
        ],
        compiler_params=_cparams(("parallel", "parallel")),
        name="qkv_a",
    )(x, g, wk, wqv_t, ck, sk, ct, st)


def _diff_attn_kernel(qt_ref, k_ref, vt_ref, lq1_ref, lk1_ref, lq2_ref, lk2_ref, g_ref, o_ref,
                      sa_ref, sb_ref, m_ref, acc_ref, *, tk, lambda_init):
    S = k_ref.shape[0]
    n = S // tk
    qt = qt_ref[...]
    row = lax.broadcasted_iota(jnp.int32, qt.shape, 0)
    comp0 = (row % HEAD_DIM) < (HEAD_DIM // 2)
    zero = jnp.zeros_like(qt)
    qts = (jnp.where(comp0, qt, zero), jnp.where(comp0, zero, qt))

    m_ref[...] = jnp.full(m_ref.shape, NEG, F32)
    acc_ref[...] = jnp.zeros(acc_ref.shape, F32)

    def scores(j, s_ref):
        kb = k_ref[pl.ds(pl.multiple_of(j * tk, tk), tk), :]
        for c in range(2):
            s_ref[c] = jnp.dot(kb, qts[c], preferred_element_type=F32)

    def softmax_pv(j, s_ref):
        vtb = vt_ref[:, pl.ds(pl.multiple_of(j * tk, tk), tk)]
        for c in range(2):
            s = s_ref[c]
            m_old = m_ref[c:c + 1, :]
            m_new = jnp.maximum(m_old, jnp.max(s, axis=0, keepdims=True))
            alpha = jnp.exp2(m_old - m_new)
            p = jnp.exp2(s - m_new).astype(BF16)
            acc_ref[c] = alpha * acc_ref[c] + jnp.dot(vtb, p, preferred_element_type=F32)
            m_ref[c:c + 1, :] = m_new

    bufs = (sa_ref, sb_ref)
    scores(0, sa_ref)

    unroll = KV_UNROLL
    while unroll > 2 and (n - 1) // unroll < 3:
        unroll //= 2

    def body(jj, carry):
        j = unroll * jj
        for u in range(unroll):
            scores(j + u + 1, bufs[(u + 1) % 2])
            softmax_pv(j + u, bufs[u % 2])
        return carry

    trips = (n - 1) // unroll
    lax.fori_loop(0, trips, body, 0)
    done = trips * unroll
    for u in range(n - done):
        if done + u + 1 < n:
            scores(done + u + 1, bufs[(u + 1) % 2])
        softmax_pv(done + u, bufs[u % 2])

    lam = (jnp.exp(jnp.sum(lq1_ref[...] * lk1_ref[...], axis=-1, keepdims=True))
           - jnp.exp(jnp.sum(lq2_ref[...] * lk2_ref[...], axis=-1, keepdims=True)) + lambda_init)
    o0 = acc_ref[0, :PAIR, :] * (1.0 / acc_ref[0, PAIR:PAIR + 1, :])
    o1 = acc_ref[1, :PAIR, :] * (1.0 / acc_ref[1, PAIR:PAIR + 1, :])
    ot = o0 - lam * o1
    ms = jnp.mean(ot * ot, axis=0, keepdims=True)
    on = (ot * lax.rsqrt(ms + NORM_EPS)).T
    o_ref[...] = (on * g_ref[...] * (1.0 - lambda_init)).astype(o_ref.dtype)


def _diff_attn(qt, k, vt, lq1, lk1, lq2, lk2, subln_g, lambda_init):
    B, nqk, S = qt.shape
    tq = min(TQ, S)
    tk = min(TK, S)
    lam_spec = _const_spec((1, HEAD_DIM))
    return pl.pallas_call(
        functools.partial(_diff_attn_kernel, tk=tk, lambda_init=lambda_init),
        grid=(B, DA_HEADS, S // tq),
        in_specs=[
            pl.BlockSpec((None, PAIR, tq), lambda b, h, i: (b, h, i)),
            pl.BlockSpec((None, S, PAIR), lambda b, h, i: (b, 0, h)),
            pl.BlockSpec((None, VT_ROWS, S), lambda b, h, i: (b, h, 0)),
            lam_spec, lam_spec, lam_spec, lam_spec,
            _const_spec((1, PAIR)),
        ],
        out_specs=pl.BlockSpec((None, tq, PAIR), lambda b, h, i: (b, i, h)),
        out_shape=jax.ShapeDtypeStruct((B, S, nqk), BF16),
        scratch_shapes=[
            pltpu.VMEM((2, tk, tq), F32),
            pltpu.VMEM((2, tk, tq), F32),
            pltpu.VMEM((8, tq), F32),
            pltpu.VMEM((2, VT_ROWS, tq), F32),
        ],
        compiler_params=_cparams(("parallel", "parallel", "arbitrary")),
        name="diff_attn",
    )(qt, k, vt, lq1, lk1, lq2, lk2, subln_g)


def _out_proj_kernel(o_ref, w_ref, g_ref, x_ref, y_ref):
    h = jnp.dot(o_ref[...], w_ref[...], preferred_element_type=F32)
    y_ref[...] = x_ref[...] + _rms(h, g_ref[...])


def _out_proj(o, w, g, x):
    B, S, D = x.shape
    tm = min(TM, S)
    tok = pl.BlockSpec((None, tm, D), lambda b, i: (b, i, 0))
    return pl.pallas_call(
        _out_proj_kernel,
        grid=(B, S // tm),
        in_specs=[tok, _const_spec((D, D)), _const_spec((1, D)), tok],
        out_specs=tok,
        out_shape=jax.ShapeDtypeStruct((B, S, D), F32),
        compiler_params=_cparams(("parallel", "parallel")),
        name="out_proj",
    )(o, w, g, x)


def _combine_out_proj_kernel(*refs):
    G = len(DL_PATTERNS)
    o_refs, e_refs = refs[:G], refs[G:2 * G]
    w_ref, g_ref, x_ref, y_ref = refs[2 * G:2 * G + 4]
    ocat_ref = refs[2 * G + 4]
    staged = [g for g, (_, dil) in enumerate(DL_PATTERNS) if dil > 1]
    buf_refs = {}
    for n, g in enumerate(staged):
        buf_refs[g] = refs[2 * G + 5 + n]
        buf_refs[G + g] = refs[2 * G + 5 + len(staged) + n]
    tm, C = x_ref.shape
    n_slab = C // LANES

    for g, (_, dil) in enumerate(DL_PATTERNS):
        if dil == 1:
            continue
        rows = tm // dil
        for ref, buf_ref in ((o_refs[g], buf_refs[g]), (e_refs[g], buf_refs[G + g])):
            for r in range(dil):
                for s in range(n_slab):
                    buf_ref[s, pl.ds(r, rows, stride=dil), :] = (
                        ref[:, r * C + s * LANES:r * C + (s + 1) * LANES].astype(F32))

    def slab(g, s, kind):
        if DL_PATTERNS[g][1] == 1:
            ref = (o_refs, e_refs)[kind][g]
            return ref[:, s * LANES:(s + 1) * LANES].astype(F32)
        return buf_refs[kind * G + g][s]

    for s in range(n_slab):
        es = [slab(g, s, 1) for g in range(G)]
        m = functools.reduce(jnp.maximum, es)
        ws = [jnp.exp2(e - m) for e in es]
        o = sum(w * slab(g, s, 0) for g, w in enumerate(ws)) * (1.0 / sum(ws))
        ocat_ref[:, s * LANES:(s + 1) * LANES] = o.astype(BF16)
    h = jnp.dot(ocat_ref[...], w_ref[...], preferred_element_type=F32)
    y_ref[...] = x_ref[...] + _rms(h, g_ref[...])


def _combine_out_proj(os, es, w, g, x):
    B, S, D = x.shape
    tm = min(TM, S)
    tok = pl.BlockSpec((None, tm, D), lambda b, i: (b, i, 0))
    views = [pl.BlockSpec((None, tm // dil, dil * D), lambda b, i: (b, i, 0)) for _, dil in DL_PATTERNS]
    return pl.pallas_call(
        _combine_out_proj_kernel,
        grid=(B, S // tm),
        in_specs=views + views + [_const_spec((D, D)), _const_spec((1, D)), tok],
        out_specs=tok,
        out_shape=jax.ShapeDtypeStruct((B, S, D), F32),
        scratch_shapes=[pltpu.VMEM((tm, D), BF16)]
        + [pltpu.VMEM((D // LANES, tm, LANES), F32)] * (2 * sum(dil > 1 for _, dil in DL_PATTERNS)),
        compiler_params=_cparams(("parallel", "parallel")),
        name="combine_out_proj",
    )(*os, *es, w, g, x)


def _ffn_kernel(x_ref, xp_ref, xn_ref, gpre_ref, wup_ref, cw_ref, cb_ref, wdn_ref, gpost_ref, y_ref,
                ha_ref, hb_ref, acc_ref):
    i = pl.program_id(1)
    last = pl.num_programs(1) - 1
    tm = x_ref.shape[0]
    x = x_ref[...]
    gpre = gpre_ref[...]
    keep_prev = (i > 0).astype(F32)
    keep_next = (i < last).astype(F32)
    xe = jnp.concatenate([_rms(xp_ref[...], gpre) * keep_prev, _rms(x, gpre),
                          _rms(xn_ref[...], gpre) * keep_next], axis=0).astype(BF16)
    acc_ref[...] = jnp.zeros(acc_ref.shape, F32)
    n_chunks = wup_ref.shape[0]

    n_slab = 2 * FF_CHUNK // LANES

    def up_proj(c, h_ref):
        h = jnp.dot(xe, wup_ref[c], preferred_element_type=F32)
        for s in range(n_slab):
            h_ref[s] = h[:, s * LANES:(s + 1) * LANES]

    def conv_act_down(c, h_ref):
        cw = cw_ref[c]
        cb = cb_ref[c]
        ys = []
        for s in range(n_slab):
            cols = slice(s * LANES, (s + 1) * LANES)
            ys.append(h_ref[s, pl.ds(7, tm), :] * cw[0:1, cols] + h_ref[s, pl.ds(8, tm), :] * cw[1:2, cols]
                      + h_ref[s, pl.ds(9, tm), :] * cw[2:3, cols] + cb[:, cols])
        gate = jnp.concatenate(ys[:n_slab // 2], axis=1)
        up = jnp.concatenate(ys[n_slab // 2:], axis=1)
        cdf = 0.5 * (1.0 + jnp.tanh(math.sqrt(2.0 / math.pi) * (gate + 0.044715 * (gate * gate * gate))))
        act = (gate * cdf * up).astype(BF16)
        acc_ref[...] += jnp.dot(act, wdn_ref[c], preferred_element_type=F32)

    bufs = (ha_ref, hb_ref)
    up_proj(0, ha_ref)

    def body(cc, carry):
        c = FF_UNROLL * cc
        for u in range(FF_UNROLL):
            up_proj(c + u + 1, bufs[(u + 1) % 2])
            conv_act_down(c + u, bufs[u % 2])
        return carry

    trips = (n_chunks - 1) // FF_UNROLL
    lax.fori_loop(0, trips, body, 0)
    done = trips * FF_UNROLL
    for u in range(n_chunks - done):
        if done + u + 1 < n_chunks:
            up_proj(done + u + 1, bufs[(u + 1) % 2])
        conv_act_down(done + u, bufs[u % 2])
    y_ref[...] = x + _rms(acc_ref[...], gpost_ref[...])


def _ffn(x, gpre, wup, cw, cb, wdn, gpost):
    B, S, D = x.shape
    tm = min(TM, S)
    nt8 = S // 8
    r8 = tm // 8
    nc = wup.shape[0]
    return pl.pallas_call(
        _ffn_kernel,
        grid=(B, S // tm),
        in_specs=[
            pl.BlockSpec((None, tm, D), lambda b, i: (b, i, 0)),
            pl.BlockSpec((None, 8, D), lambda b, i: (b, jnp.maximum(i * r8 - 1, 0), 0)),
            pl.BlockSpec((None, 8, D), lambda b, i: (b, jnp.minimum((i + 1) * r8, nt8 - 1), 0)),
            _const_spec((1, D)),
            _const_spec((nc, D, 2 * FF_CHUNK)),
            _const_spec((nc, 3, 2 * FF_CHUNK)),
            _const_spec((nc, 1, 2 * FF_CHUNK)),
            _const_spec((nc, FF_CHUNK, D)),
            _const_spec((1, D)),
        ],
        out_specs=pl.BlockSpec((None, tm, D), lambda b, i: (b, i, 0)),
        out_shape=jax.ShapeDtypeStruct((B, S, D), F32),
        scratch_shapes=[
            pltpu.VMEM((2 * FF_CHUNK // LANES, tm + 16, LANES), F32),
            pltpu.VMEM((2 * FF_CHUNK // LANES, tm + 16, LANES), F32),
            pltpu.VMEM((tm, D), F32),
        ],
        compiler_params=_cparams(("parallel", "parallel")),
        name="conv_ffn",
    )(x, x, x, gpre, wup, cw, cb, wdn, gpost)


def _qkv_b_kernel(x_ref, g_ref, w_ref, ck_ref, sk_ref, *refs):
    G = len(DL_PATTERNS)
    q_refs, k_refs, v_refs = refs[:G], refs[G:2 * G], refs[2 * G:3 * G]
    ybuf_ref = refs[3 * G]
    tm = x_ref.shape[0]
    C = D_MODEL
    n_slab = C // LANES
    xn = _rms(x_ref[...], g_ref[...]).astype(BF16)
    ck = ck_ref[...]
    sk = sk_ref[...]

    def emit(slabs, dsts):
        if any(dil > 1 for _, dil in dsts):
            for s in range(n_slab):
                ybuf_ref[s] = slabs[s]
        for ref, dil in dsts:
            if dil == 1:
                for s in range(n_slab):
                    ref[:, s * LANES:(s + 1) * LANES] = slabs[s].astype(BF16)
            else:
                rows = tm // dil
                for r in range(dil):
                    for s in range(n_slab):
                        ref[:, r * C + s * LANES:r * C + (s + 1) * LANES] = (
                            ybuf_ref[s, pl.ds(r, rows, stride=dil), :].astype(BF16))

    for c in range(2 * G + 1):
        y = jnp.dot(xn, w_ref[:, c * C:(c + 1) * C], preferred_element_type=F32)
        slabs = [y[:, s * LANES:(s + 1) * LANES] for s in range(n_slab)]
        if c < 2 * G:
            scale = Q_SCALE if c < G else 1.0
            slabs = [yh * ck + pltpu.roll(yh, HEAD_DIM, 1) * sk for yh in slabs]
            if scale != 1.0:
                slabs = [yh * scale for yh in slabs]
            g = c % G
            emit(slabs, [((q_refs if c < G else k_refs)[g], DL_PATTERNS[g][1])])
        else:
            emit(slabs, [(v_refs[g], DL_PATTERNS[g][1]) for g in range(G)])


def _qkv_b(x, g, w, ck, sk):
    B, S, D = x.shape
    tm = min(TM, S)
    G = len(DL_PATTERNS)
    C = DL_HEADS * HEAD_DIM
    out_specs, out_shape = [], []
    for _ in range(3):
        for _, dil in DL_PATTERNS:
            out_specs.append(pl.BlockSpec((None, tm // dil, dil * C), lambda b, i: (b, i, 0)))
            out_shape.append(jax.ShapeDtypeStruct((B, S // dil, dil * C), BF16))
    return pl.pallas_call(
        _qkv_b_kernel,
        grid=(B, S // tm),
        in_specs=[
            pl.BlockSpec((None, tm, D), lambda b, i: (b, i, 0)),
            _const_spec((1, D)),
            _const_spec((D, (2 * G + 1) * C)),
            pl.BlockSpec((tm, PAIR), lambda b, i: (i, 0)),
            pl.BlockSpec((tm, PAIR), lambda b, i: (i, 0)),
        ],
        out_specs=out_specs,
        out_shape=out_shape,
        scratch_shapes=[pltpu.VMEM((C // LANES, tm, LANES), F32)],
        compiler_params=_cparams(("parallel", "parallel")),
        name="qkv_b",
    )(x, g, w, ck, sk)


def _dil_attn_kernel(q_ref, k_ref, kp_ref, kn_ref, v_ref, vp_ref, vn_ref, o_ref, e_ref, kx_ref, vx_ref):
    i = pl.program_id(2)
    last = pl.num_programs(2) - 1
    tq = q_ref.shape[0]
    H = DIL_HALF
    kx_ref[0:H, :] = kp_ref[...]
    kx_ref[H:H + tq, :] = k_ref[...]
    kx_ref[H + tq:, :] = kn_ref[...]
    vx_ref[0:H, :] = vp_ref[...]
    vx_ref[H:H + tq, :] = v_ref[...]
    vx_ref[H + tq:, :] = vn_ref[...]

    sub = DIL_SUB
    win = sub + 2 * H
    n_sub = tq // sub
    jq = lax.broadcasted_iota(jnp.int32, (sub, win), 0)
    ck = lax.broadcasted_iota(jnp.int32, (sub, win), 1)
    band = (ck >= jq) & (ck <= jq + 2 * H)
    band_first = band & (ck >= jnp.where(i > 0, 0, H))
    band_last = band & (ck < jnp.where(i < last, win, win - H))
    lane = lax.broadcasted_iota(jnp.int32, (sub, PAIR), 1)
    head_a = (lane % HEAD_DIM) < (HEAD_DIM // 2)
    out_a = lane < HEAD_DIM

    for s in range(n_sub):
        valid = band_first if s == 0 else (band_last if s == n_sub - 1 else band)
        if n_sub == 1:
            valid = band_first & band_last
        for hp in range(q_ref.shape[1] // PAIR):
            cols = slice(hp * PAIR, (hp + 1) * PAIR)
            qp = q_ref[s * sub:(s + 1) * sub, cols]
            kw = kx_ref[s * sub:s * sub + win, cols]
            vw = vx_ref[s * sub:s * sub + win, cols]
            zero = jnp.zeros_like(qp)
            q2 = jnp.concatenate([jnp.where(head_a, qp, zero), jnp.where(head_a, zero, qp)], axis=0)
            sc2 = lax.dot_general(q2, kw, (((1,), (1,)), ((), ())), preferred_element_type=F32)
            ms, ls, ps = [], [], []
            for hh in range(2):
                sc = jnp.where(valid, sc2[hh * sub:(hh + 1) * sub], NEG)
                m = jnp.max(sc, axis=-1, keepdims=True)
                p = jnp.exp2(sc - m)
                ms.append(m)
                ls.append(jnp.sum(p, axis=-1, keepdims=True))
                ps.append(p.astype(BF16))
            o2 = jnp.dot(jnp.concatenate(ps, axis=0), vw, preferred_element_type=F32)
            l = jnp.where(out_a, ls[0], ls[1])
            o = jnp.where(out_a, o2[:sub], o2[sub:]) * (1.0 / l)
            o_ref[s * sub:(s + 1) * sub, cols] = o.astype(o_ref.dtype)
            e_ref[s * sub:(s + 1) * sub, cols] = jnp.where(out_a, ms[0], ms[1]) + jnp.log2(l)


def _dil_attn(q, k, v, dil):
    B, L, w = q.shape
    C = DL_HEADS * HEAD_DIM
    assert w == dil * C
    tq = min(DIL_TQ, L)
    H = DIL_HALF
    nh = L // H
    rq = tq // H
    main = lambda b, r, i: (b, i, r)
    prev = lambda b, r, i: (b, jnp.maximum(i * rq - 1, 0), r)
    nxt = lambda b, r, i: (b, jnp.minimum((i + 1) * rq, nh - 1), r)
    return pl.pallas_call(
        _dil_attn_kernel,
        grid=(B, dil, L // tq),
        in_specs=[
            pl.BlockSpec((None, tq, C), main),
            pl.BlockSpec((None, tq, C), main),
            pl.BlockSpec((None, H, C), prev),
            pl.BlockSpec((None, H, C), nxt),
            pl.BlockSpec((None, tq, C), main),
            pl.BlockSpec((None, H, C), prev),
            pl.BlockSpec((None, H, C), nxt),
        ],
        out_specs=[
            pl.BlockSpec((None, tq, C), main),
            pl.BlockSpec((None, tq, C), main),
        ],
        out_shape=[
            jax.ShapeDtypeStruct((B, L, dil * C), BF16),
            jax.ShapeDtypeStruct((B, L, dil * C), F32),
        ],
        scratch_shapes=[
            pltpu.VMEM((tq + 2 * H, C), BF16),
            pltpu.VMEM((tq + 2 * H, C), BF16),
        ],
        compiler_params=_cparams(("parallel", "parallel", "parallel")),
        name=f"dil_attn_{dil}",
    )(q, k, k, k, v, v, v)


def _rope_tables(S):
    inv = 1.0 / (ROPE_THETA ** (jnp.arange(0, HEAD_DIM, 2, dtype=F32) / HEAD_DIM))
    ang = jnp.arange(S, dtype=F32)[:, None] * inv[None, :]
    cos, sin = jnp.cos(ang), jnp.sin(ang)
    ck = jnp.concatenate([cos] * 4, axis=1)
    sk = jnp.concatenate([-sin, -sin, sin, sin], axis=1)
    ct = jnp.concatenate([cos.T, cos.T], axis=0)
    st = jnp.concatenate([sin.T, sin.T], axis=0)
    return ck, sk, ct, st


def _prep_ffn(w_up, conv_w, conv_b, w_down):
    nc = D_FF // FF_CHUNK
    D = w_up.shape[0]

    def chunk_cols(a):
        g = a[..., :D_FF].reshape(a.shape[:-1] + (nc, FF_CHUNK))
        u = a[..., D_FF:].reshape(a.shape[:-1] + (nc, FF_CHUNK))
        return jnp.moveaxis(jnp.concatenate([g, u], axis=-1), -2, 0)

    return (chunk_cols(w_up).astype(BF16), chunk_cols(conv_w), chunk_cols(conv_b[None, :]),
            w_down.reshape(nc, FF_CHUNK, D).astype(BF16))


def _trunk(x, p):
    B, S, D = x.shape
    ck, sk, ct, st = _rope_tables(S)
    row = lambda a: a.reshape(1, -1)
    for i in range(DEPTH):
        j = i // N_MIXERS
        if i % N_MIXERS == 0:
            lambda_init = 0.8 - 0.6 * math.exp(-0.3 * i)
            n = DA_HEADS * PAIR
            perm = _rope_perm(n)
            w_in = p['da_w_in'][j]
            wk = w_in[:, n:2 * n][:, perm].astype(BF16)
            wqv_t = jnp.concatenate([w_in[:, :n][:, perm], w_in[:, 2 * n:]], axis=1).T.astype(BF16)
            k, qt, vt = _qkv_a(x, row(p['norm_mix_pre'][i]), wk, wqv_t, ck, sk, ct, st)
            o = _diff_attn(qt, k, vt, row(p['da_lambda_q1'][j]), row(p['da_lambda_k1'][j]),
                           row(p['da_lambda_q2'][j]), row(p['da_lambda_k2'][j]),
                           row(p['da_subln'][j]), lambda_init)
            x = _out_proj(o, p['da_w_out'][j].astype(BF16), row(p['norm_mix_post'][i]), x)
        else:
            G = len(DL_PATTERNS)
            nq = G * DL_HEADS * HEAD_DIM
            perm = _rope_perm(nq)
            w_in = p['dl_w_in'][j]
            w = jnp.concatenate([w_in[:, :nq][:, perm], w_in[:, nq:2 * nq][:, perm], w_in[:, 2 * nq:]],
                                axis=1).astype(BF16)
            qkv = _qkv_b(x, row(p['norm_mix_pre'][i]), w, ck, sk)
            os, es = [], []
            for g, (window, dil) in enumerate(DL_PATTERNS):
                assert window // (2 * dil) == DIL_HALF
                o_g, e_g = _dil_attn(qkv[g], qkv[G + g], qkv[2 * G + g], dil)
                os.append(o_g)
                es.append(e_g)
            x = _combine_out_proj(os, es, p['dl_w_out'][j].astype(BF16), row(p['norm_mix_post'][i]), x)
        wup, cw, cb, wdn = _prep_ffn(p['ffn_w_up'][i], p['ffn_conv_w'][i], p['ffn_conv_b'][i], p['ffn_w_down'][i])
        x = _ffn(x, row(p['norm_ffn_pre'][i]), wup, cw, cb, wdn, row(p['norm_ffn_post'][i]))
    return x


def kernel(x_prompt, x_sample, norm_mix_pre, norm_mix_post, norm_ffn_pre, norm_ffn_post, da_w_in, da_w_out, da_lambda_q1, da_lambda_k1, da_lambda_q2, da_lambda_k2, da_subln, dl_w_in, dl_w_out, ffn_w_up, ffn_conv_w, ffn_conv_b, ffn_w_down):
    p = dict(norm_mix_pre=norm_mix_pre, norm_mix_post=norm_mix_post, norm_ffn_pre=norm_ffn_pre,
             norm_ffn_post=norm_ffn_post, da_w_in=da_w_in, da_w_out=da_w_out, da_lambda_q1=da_lambda_q1,
             da_lambda_k1=da_lambda_k1, da_lambda_q2=da_lambda_q2, da_lambda_k2=da_lambda_k2,
             da_subln=da_subln, dl_w_in=dl_w_in, dl_w_out=dl_w_out, ffn_w_up=ffn_w_up,
             ffn_conv_w=ffn_conv_w, ffn_conv_b=ffn_conv_b, ffn_w_down=ffn_w_down)
    return (_trunk(x_prompt, p), _trunk(x_sample, p))
```

```python
import functools
import math

import numpy as np
import jax
import jax.numpy as jnp
from jax import lax
from jax.experimental import pallas as pl
from jax.experimental.pallas import tpu as pltpu

D_MODEL = 1024
DEPTH = 2
N_MIXERS = 2
DA_HEADS = 8
HEAD_DIM = 64
DL_HEADS = 16
DL_PATTERNS = ((128, 1), (512, 4), (2048, 16))
D_FF = 2816
ROPE_THETA = 10000.0
NORM_EPS = 1e-6
NEG = -1e30

LANES = 128
PAIR = 2 * HEAD_DIM
VMEM_LIMIT = 56 * 1024 * 1024
Q_SCALE = (HEAD_DIM ** -0.5) * math.log2(math.e)

TM = 512
TQ = 512
TK = 512
KV_UNROLL = 8
FF_UNROLL = 4
Q_TILES_PER_STEP = 2
FF_CHUNK = 256
DIL_TQ = 1024
DIL_SUB = 128
DIL_HALF = 64
VT_ROWS = PAIR + 16

BF16 = jnp.bfloat16
F32 = jnp.float32


def _cparams(sem):
    return pltpu.CompilerParams(dimension_semantics=sem, vmem_limit_bytes=VMEM_LIMIT)


def _const_spec(shape):
    nd = len(shape)
    return pl.BlockSpec(shape, lambda *_: (0,) * nd, pipeline_mode=pl.Buffered(1))


def _rms(x, g):
    ms = jnp.mean(x * x, axis=-1, keepdims=True)
    return x * lax.rsqrt(ms + NORM_EPS) * g


def _rope_perm(n):
    return np.arange(n).reshape(n // PAIR, 2, 2, HEAD_DIM // 2).transpose(0, 2, 1, 3).reshape(-1)


def _qkv_a_kernel(x_ref, g_ref, wk_ref, wqv_ref, ck_ref, sk_ref, ct_ref, st_ref,
                  k_ref, qt_ref, vt_ref):
    xn = _rms(x_ref[...], g_ref[...]).astype(BF16)
    k = jnp.dot(xn, wk_ref[...], preferred_element_type=F32)
    ck = ck_ref[...]
    sk = sk_ref[...]
    for h in range(DA_HEADS):
        kh = k[:, h * PAIR:(h + 1) * PAIR]
        k_ref[:, h * PAIR:(h + 1) * PAIR] = (kh * ck + pltpu.roll(kh, HEAD_DIM, 1) * sk).astype(BF16)
    qvt = lax.dot_general(wqv_ref[...], xn, (((1,), (1,)), ((), ())), preferred_element_type=F32)
    ct = ct_ref[...]
    st = st_ref[...]
    for h in range(DA_HEADS):
        x1 = qvt[h * PAIR:h * PAIR + HEAD_DIM]
        x2 = qvt[h * PAIR + HEAD_DIM:(h + 1) * PAIR]
        qt_ref[h * PAIR:h * PAIR + HEAD_DIM, :] = ((x1 * ct - x2 * st) * Q_SCALE).astype(BF16)
        qt_ref[h * PAIR + HEAD_DIM:(h + 1) * PAIR, :] = ((x1 * st + x2 * ct) * Q_SCALE).astype(BF16)
    ones = jnp.ones((VT_ROWS - PAIR, vt_ref.shape[1]), BF16)
    for h in range(DA_HEADS):
        vt_ref[h * VT_ROWS:h * VT_ROWS + PAIR, :] = qvt[D_MODEL + h * PAIR:D_MODEL + (h + 1) * PAIR].astype(BF16)
        vt_ref[h * VT_ROWS + PAIR:(h + 1) * VT_ROWS, :] = ones


def _qkv_a(x, g, wk, wqv_t, ck, sk, ct, st):
    B, S, D = x.shape
    tm = min(TM, S)
    nqk = DA_HEADS * PAIR
    nvt = DA_HEADS * VT_ROWS
    return pl.pallas_call(
        _qkv_a_kernel,
        grid=(B, S // tm),
        in_specs=[
            pl.BlockSpec((None, tm, D), lambda b, i: (b, i, 0)),
            _const_spec((1, D)),
            _const_spec((D, nqk)),
            _const_spec((2 * nqk, D)),
            pl.BlockSpec((tm, PAIR), lambda b, i: (i, 0)),
            pl.BlockSpec((tm, PAIR), lambda b, i: (i, 0)),
            pl.BlockSpec((HEAD_DIM, tm), lambda b, i: (0, i)),
            pl.BlockSpec((HEAD_DIM, tm), lambda b, i: (0, i)),
        ],
        out_specs=[
            pl.BlockSpec((None, tm, nqk), lambda b, i: (b, i, 0)),
            pl.BlockSpec((None, nqk, tm), lambda b, i: (b, 0, i)),
            pl.BlockSpec((None, nvt, tm), lambda b, i: (b, 0, i)),
        ],
        out_shape=[
            jax.ShapeDtypeStruct((B, S, nqk), BF16),
            jax.ShapeDtypeStruct((B, nqk, S), BF16),
            jax.ShapeDtypeStruct((B, nvt, S), BF16),
        ],
        compiler_params=_cparams(("parallel", "parallel")),
        name="qkv_a",
    )(x, g, wk, wqv_t, ck, sk, ct, st)


def _diff_attn_kernel(qt_ref, k_ref, vt_ref, lq1_ref, lk1_ref, lq2_ref, lk2_ref, g_ref, o_ref,
                      *scratch, tq, tk, lambda_init):
    S = k_ref.shape[0]
    n = S // tk
    lam = (jnp.exp(jnp.sum(lq1_ref[...] * lk1_ref[...], axis=-1, keepdims=True))
           - jnp.exp(jnp.sum(lq2_ref[...] * lk2_ref[...], axis=-1, keepdims=True)) + lambda_init)

    unroll = KV_UNROLL
    while unroll > 2 and (n - 1) // unroll < 3:
        unroll //= 2
    trips = (n - 1) // unroll

    def one_tile(t, sa_ref, sb_ref, m_ref, acc_ref):
        qt = qt_ref[:, t * tq:(t + 1) * tq]
        row = lax.broadcasted_iota(jnp.int32, qt.shape, 0)
        comp0 = (row % HEAD_DIM) < (HEAD_DIM // 2)
        zero = jnp.zeros_like(qt)
        qts = (jnp.where(comp0, qt, zero), jnp.where(comp0, zero, qt))

        m_ref[...] = jnp.full(m_ref.shape, NEG, F32)
        acc_ref[...] = jnp.zeros(acc_ref.shape, F32)

        def scores(j, s_ref):
            kb = k_ref[pl.ds(pl.multiple_of(j * tk, tk), tk), :]
            for c in range(2):
                s_ref[c] = jnp.dot(kb, qts[c], preferred_element_type=F32)

        def softmax_pv(j, s_ref):
            vtb = vt_ref[:, pl.ds(pl.multiple_of(j * tk, tk), tk)]
            for c in range(2):
                s = s_ref[c]
                m_old = m_ref[c:c + 1, :]
                m_new = jnp.maximum(m_old, jnp.max(s, axis=0, keepdims=True))
                alpha = jnp.exp2(m_old - m_new)
                p = jnp.exp2(s - m_new).astype(BF16)
                acc_ref[c] = alpha * acc_ref[c] + jnp.dot(vtb, p, preferred_element_type=F32)
                m_ref[c:c + 1, :] = m_new

        bufs = (sa_ref, sb_ref)
        scores(0, sa_ref)

        def body(jj, carry):
            j = unroll * jj
            for u in range(unroll):
                scores(j + u + 1, bufs[(u + 1) % 2])
                softmax_pv(j + u, bufs[u % 2])
            return carry

        lax.fori_loop(0, trips, body, 0)
        done = trips * unroll
        for u in range(n - done):
            if done + u + 1 < n:
                scores(done + u + 1, bufs[(u + 1) % 2])
            softmax_pv(done + u, bufs[u % 2])

        o0 = acc_ref[0, :PAIR, :] * (1.0 / acc_ref[0, PAIR:PAIR + 1, :])
        o1 = acc_ref[1, :PAIR, :] * (1.0 / acc_ref[1, PAIR:PAIR + 1, :])
        ot = o0 - lam * o1
        ms = jnp.mean(ot * ot, axis=0, keepdims=True)
        on = (ot * lax.rsqrt(ms + NORM_EPS)).T
        o_ref[t * tq:(t + 1) * tq, :] = (on * g_ref[...] * (1.0 - lambda_init)).astype(o_ref.dtype)

    for t in range(len(scratch) // 4):
        one_tile(t, *scratch[4 * t:4 * t + 4])


def _diff_attn(qt, k, vt, lq1, lk1, lq2, lk2, subln_g, lambda_init):
    B, nqk, S = qt.shape
    tq = min(TQ, S)
    tk = min(TK, S)
    nt = Q_TILES_PER_STEP if S % (Q_TILES_PER_STEP * tq) == 0 else 1
    lam_spec = _const_spec((1, HEAD_DIM))
    tile_scratch = [
        pltpu.VMEM((2, tk, tq), F32),
        pltpu.VMEM((2, tk, tq), F32),
        pltpu.VMEM((8, tq), F32),
        pltpu.VMEM((2, VT_ROWS, tq), F32),
    ]
    return pl.pallas_call(
        functools.partial(_diff_attn_kernel, tq=tq, tk=tk, lambda_init=lambda_init),
        grid=(B, DA_HEADS, S // (nt * tq)),
        in_specs=[
            pl.BlockSpec((None, PAIR, nt * tq), lambda b, h, i: (b, h, i)),
            pl.BlockSpec((None, S, PAIR), lambda b, h, i: (b, 0, h)),
            pl.BlockSpec((None, VT_ROWS, S), lambda b, h, i: (b, h, 0)),
            lam_spec, lam_spec, lam_spec, lam_spec,
            _const_spec((1, PAIR)),
        ],
        out_specs=pl.BlockSpec((None, nt * tq, PAIR), lambda b, h, i: (b, i, h)),
        out_shape=jax.ShapeDtypeStruct((B, S, nqk), BF16),
        scratch_shapes=tile_scratch * nt,
        compiler_params=_cparams(("parallel", "parallel", "arbitrary")),
        name="diff_attn",
    )(qt, k, vt, lq1, lk1, lq2, lk2, subln_g)


def _out_proj_kernel(o_ref, w_ref, g_ref, x_ref, y_ref):
    h = jnp.dot(o_ref[...], w_ref[...], preferred_element_type=F32)
    y_ref[...] = x_ref[...] + _rms(h, g_ref[...])


def _out_proj(o, w, g, x):
    B, S, D = x.shape
    tm = min(TM, S)
    tok = pl.BlockSpec((None, tm, D), lambda b, i: (b, i, 0))
    return pl.pallas_call(
        _out_proj_kernel,
        grid=(B, S // tm),
        in_specs=[tok, _const_spec((D, D)), _const_spec((1, D)), tok],
        out_specs=tok,
        out_shape=jax.ShapeDtypeStruct((B, S, D), F32),
        compiler_params=_cparams(("parallel", "parallel")),
        name="out_proj",
    )(o, w, g, x)


def _combine_out_proj_kernel(*refs):
    G = len(DL_PATTERNS)
    o_refs, e_refs = refs[:G], refs[G:2 * G]
    w_ref, g_ref, x_ref, y_ref = refs[2 * G:2 * G + 4]
    ocat_ref = refs[2 * G + 4]
    staged = [g for g, (_, dil) in enumerate(DL_PATTERNS) if dil > 1]
    buf_refs = {}
    for n, g in enumerate(staged):
        buf_refs[g] = refs[2 * G + 5 + n]
        buf_refs[G + g] = refs[2 * G + 5 + len(staged) + n]
    tm, C = x_ref.shape
    n_slab = C // LANES

    for g, (_, dil) in enumerate(DL_PATTERNS):
        if dil == 1:
            continue
        rows = tm // dil
        for ref, buf_ref in ((o_refs[g], buf_refs[g]), (e_refs[g], buf_refs[G + g])):
            for r in range(dil):
                for s in range(n_slab):
                    buf_ref[s, pl.ds(r, rows, stride=dil), :] = (
                        ref[:, r * C + s * LANES:r * C + (s + 1) * LANES].astype(F32))

    def slab(g, s, kind):
        if DL_PATTERNS[g][1] == 1:
            ref = (o_refs, e_refs)[kind][g]
            return ref[:, s * LANES:(s + 1) * LANES].astype(F32)
        return buf_refs[kind * G + g][s]

    for s in range(n_slab):
        es = [slab(g, s, 1) for g in range(G)]
        m = functools.reduce(jnp.maximum, es)
        ws = [jnp.exp2(e - m) for e in es]
        o = sum(w * slab(g, s, 0) for g, w in enumerate(ws)) * (1.0 / sum(ws))
        ocat_ref[:, s * LANES:(s + 1) * LANES] = o.astype(BF16)
    h = jnp.dot(ocat_ref[...], w_ref[...], preferred_element_type=F32)
    y_ref[...] = x_ref[...] + _rms(h, g_ref[...])


def _combine_out_proj(os, es, w, g, x):
    B, S, D = x.shape
    tm = min(TM, S)
    tok = pl.BlockSpec((None, tm, D), lambda b, i: (b, i, 0))
    views = [pl.BlockSpec((None, tm // dil, dil * D), lambda b, i: (b, i, 0)) for _, dil in DL_PATTERNS]
    return pl.pallas_call(
        _combine_out_proj_kernel,
        grid=(B, S // tm),
        in_specs=views + views + [_const_spec((D, D)), _const_spec((1, D)), tok],
        out_specs=tok,
        out_shape=jax.ShapeDtypeStruct((B, S, D), F32),
        scratch_shapes=[pltpu.VMEM((tm, D), BF16)]
        + [pltpu.VMEM((D // LANES, tm, LANES), F32)] * (2 * sum(dil > 1 for _, dil in DL_PATTERNS)),
        compiler_params=_cparams(("parallel", "parallel")),
        name="combine_out_proj",
    )(*os, *es, w, g, x)


def _ffn_kernel(x_ref, xp_ref, xn_ref, gpre_ref, wup_ref, cw_ref, cb_ref, wdn_ref, gpost_ref, y_ref,
                ha_ref, hb_ref, acc_ref):
    i = pl.program_id(1)
    last = pl.num_programs(1) - 1
    tm = x_ref.shape[0]
    x = x_ref[...]
    gpre = gpre_ref[...]
    keep_prev = (i > 0).astype(F32)
    keep_next = (i < last).astype(F32)
    xe = jnp.concatenate([_rms(xp_ref[...], gpre) * keep_prev, _rms(x, gpre),
                          _rms(xn_ref[...], gpre) * keep_next], axis=0).astype(BF16)
    acc_ref[...] = jnp.zeros(acc_ref.shape, F32)
    n_chunks = wup_ref.shape[0]

    n_slab = 2 * FF_CHUNK // LANES

    def up_proj(c, h_ref):
        h = jnp.dot(xe, wup_ref[c], preferred_element_type=F32)
        for s in range(n_slab):
            h_ref[s] = h[:, s * LANES:(s + 1) * LANES]

    def conv_act_down(c, h_ref):
        cw = cw_ref[c]
        cb = cb_ref[c]
        ys = []
        for s in range(n_slab):
            cols = slice(s * LANES, (s + 1) * LANES)
            ys.append(h_ref[s, pl.ds(7, tm), :] * cw[0:1, cols] + h_ref[s, pl.ds(8, tm), :] * cw[1:2, cols]
                      + h_ref[s, pl.ds(9, tm), :] * cw[2:3, cols] + cb[:, cols])
        gate = jnp.concatenate(ys[:n_slab // 2], axis=1)
        up = jnp.concatenate(ys[n_slab // 2:], axis=1)
        cdf = 0.5 * (1.0 + jnp.tanh(math.sqrt(2.0 / math.pi) * (gate + 0.044715 * (gate * gate * gate))))
        act = (gate * cdf * up).astype(BF16)
        acc_ref[...] += jnp.dot(act, wdn_ref[c], preferred_element_type=F32)

    bufs = (ha_ref, hb_ref)
    up_proj(0, ha_ref)

    def body(cc, carry):
        c = FF_UNROLL * cc
        for u in range(FF_UNROLL):
            up_proj(c + u + 1, bufs[(u + 1) % 2])
            conv_act_down(c + u, bufs[u % 2])
        return carry

    trips = (n_chunks - 1) // FF_UNROLL
    lax.fori_loop(0, trips, body, 0)
    done = trips * FF_UNROLL
    for u in range(n_chunks - done):
        if done + u + 1 < n_chunks:
            up_proj(done + u + 1, bufs[(u + 1) % 2])
        conv_act_down(done + u, bufs[u % 2])
    y_ref[...] = x + _rms(acc_ref[...], gpost_ref[...])


def _ffn(x, gpre, wup, cw, cb, wdn, gpost):
    B, S, D = x.shape
    tm = min(TM, S)
    nt8 = S // 8
    r8 = tm // 8
    nc = wup.shape[0]
    return pl.pallas_call(
        _ffn_kernel,
        grid=(B, S // tm),
        in_specs=[
            pl.BlockSpec((None, tm, D), lambda b, i: (b, i, 0)),
            pl.BlockSpec((None, 8, D), lambda b, i: (b, jnp.maximum(i * r8 - 1, 0), 0)),
            pl.BlockSpec((None, 8, D), lambda b, i: (b, jnp.minimum((i + 1) * r8, nt8 - 1), 0)),
            _const_spec((1, D)),
            _const_spec((nc, D, 2 * FF_CHUNK)),
            _const_spec((nc, 3, 2 * FF_CHUNK)),
            _const_spec((nc, 1, 2 * FF_CHUNK)),
            _const_spec((nc, FF_CHUNK, D)),
            _const_spec((1, D)),
        ],
        out_specs=pl.BlockSpec((None, tm, D), lambda b, i: (b, i, 0)),
        out_shape=jax.ShapeDtypeStruct((B, S, D), F32),
        scratch_shapes=[
            pltpu.VMEM((2 * FF_CHUNK // LANES, tm + 16, LANES), F32),
            pltpu.VMEM((2 * FF_CHUNK // LANES, tm + 16, LANES), F32),
            pltpu.VMEM((tm, D), F32),
        ],
        compiler_params=_cparams(("parallel", "parallel")),
        name="conv_ffn",
    )(x, x, x, gpre, wup, cw, cb, wdn, gpost)


def _qkv_b_kernel(x_ref, g_ref, w_ref, ck_ref, sk_ref, *refs):
    G = len(DL_PATTERNS)
    q_refs, k_refs, v_refs = refs[:G], refs[G:2 * G], refs[2 * G:3 * G]
    ybuf_ref = refs[3 * G]
    tm = x_ref.shape[0]
    C = D_MODEL
    n_slab = C // LANES
    xn = _rms(x_ref[...], g_ref[...]).astype(BF16)

    def emit(slabs, dsts):
        if any(dil > 1 for _, dil in dsts):
            for s in range(n_slab):
                ybuf_ref[s] = slabs[s]
        for ref, dil in dsts:
            if dil == 1:
                for s in range(n_slab):
                    ref[:, s * LANES:(s + 1) * LANES] = slabs[s].astype(BF16)
            else:
                rows = tm // dil
                for r in range(dil):
                    for s in range(n_slab):
                        ref[:, r * C + s * LANES:r * C + (s + 1) * LANES] = (
                            ybuf_ref[s, pl.ds(r, rows, stride=dil), :].astype(BF16))

    for c in range(2 * G + 1):
        y = jnp.dot(xn, w_ref[:, c * C:(c + 1) * C], preferred_element_type=F32)
        slabs = [y[:, s * LANES:(s + 1) * LANES] for s in range(n_slab)]
        if c < 2 * G:
            scale = Q_SCALE if c < G else 1.0
            slabs = [yh * ck_ref[...] + pltpu.roll(yh, HEAD_DIM, 1) * sk_ref[...] for yh in slabs]
            if scale != 1.0:
                slabs = [yh * scale for yh in slabs]
            g = c % G
            emit(slabs, [((q_refs if c < G else k_refs)[g], DL_PATTERNS[g][1])])
        else:
            emit(slabs, [(v_refs[g], DL_PATTERNS[g][1]) for g in range(G)])


def _qkv_b(x, g, w, ck, sk):
    B, S, D = x.shape
    tm = min(TM, S)
    G = len(DL_PATTERNS)
    C = DL_HEADS * HEAD_DIM
    out_specs, out_shape = [], []
    for _ in range(3):
        for _, dil in DL_PATTERNS:
            out_specs.append(pl.BlockSpec((None, tm // dil, dil * C), lambda b, i: (b, i, 0)))
            out_shape.append(jax.ShapeDtypeStruct((B, S // dil, dil * C), BF16))
    return pl.pallas_call(
        _qkv_b_kernel,
        grid=(B, S // tm),
        in_specs=[
            pl.BlockSpec((None, tm, D), lambda b, i: (b, i, 0)),
            _const_spec((1, D)),
            _const_spec((D, (2 * G + 1) * C)),
            pl.BlockSpec((tm, PAIR), lambda b, i: (i, 0)),
            pl.BlockSpec((tm, PAIR), lambda b, i: (i, 0)),
        ],
        out_specs=out_specs,
        out_shape=out_shape,
        scratch_shapes=[pltpu.VMEM((C // LANES, tm, LANES), F32)],
        compiler_params=_cparams(("parallel", "parallel")),
        name="qkv_b",
    )(x, g, w, ck, sk)


def _dil_attn_kernel(q_ref, k_ref, kp_ref, kn_ref, v_ref, vp_ref, vn_ref, o_ref, e_ref, kx_ref, vx_ref):
    i = pl.program_id(2)
    last = pl.num_programs(2) - 1
    tq = q_ref.shape[0]
    H = DIL_HALF
    kx_ref[0:H, :] = kp_ref[...]
    kx_ref[H:H + tq, :] = k_ref[...]
    kx_ref[H + tq:, :] = kn_ref[...]
    vx_ref[0:H, :] = vp_ref[...]
    vx_ref[H:H + tq, :] = v_ref[...]
    vx_ref[H + tq:, :] = vn_ref[...]

    sub = DIL_SUB
    win = sub + 2 * H
    n_sub = tq // sub
    jq = lax.broadcasted_iota(jnp.int32, (sub, win), 0)
    ck = lax.broadcasted_iota(jnp.int32, (sub, win), 1)
    band = (ck >= jq) & (ck <= jq + 2 * H)
    band_first = band & (ck >= jnp.where(i > 0, 0, H))
    band_last = band & (ck < jnp.where(i < last, win, win - H))
    lane = lax.broadcasted_iota(jnp.int32, (sub, PAIR), 1)
    head_a = (lane % HEAD_DIM) < (HEAD_DIM // 2)
    out_a = lane < HEAD_DIM

    for s in range(n_sub):
        valid = band_first if s == 0 else (band_last if s == n_sub - 1 else band)
        if n_sub == 1:
            valid = band_first & band_last
        for hp in range(q_ref.shape[1] // PAIR):
            cols = slice(hp * PAIR, (hp + 1) * PAIR)
            qp = q_ref[s * sub:(s + 1) * sub, cols]
            kw = kx_ref[s * sub:s * sub + win, cols]
            vw = vx_ref[s * sub:s * sub + win, cols]
            zero = jnp.zeros_like(qp)
            q2 = jnp.concatenate([jnp.where(head_a, qp, zero), jnp.where(head_a, zero, qp)], axis=0)
            sc2 = lax.dot_general(q2, kw, (((1,), (1,)), ((), ())), preferred_element_type=F32)
            ms, ls, ps = [], [], []
            for hh in range(2):
                sc = jnp.where(valid, sc2[hh * sub:(hh + 1) * sub], NEG)
                m = jnp.max(sc, axis=-1, keepdims=True)
                p = jnp.exp2(sc - m)
                ms.append(m)
                ls.append(jnp.sum(p, axis=-1, keepdims=True))
                ps.append(p.astype(BF16))
            o2 = jnp.dot(jnp.concatenate(ps, axis=0), vw, preferred_element_type=F32)
            l = jnp.where(out_a, ls[0], ls[1])
            o = jnp.where(out_a, o2[:sub], o2[sub:]) * (1.0 / l)
            o_ref[s * sub:(s + 1) * sub, cols] = o.astype(o_ref.dtype)
            e_ref[s * sub:(s + 1) * sub, cols] = jnp.where(out_a, ms[0], ms[1]) + jnp.log2(l)


def _dil_attn(q, k, v, dil):
    B, L, w = q.shape
    C = DL_HEADS * HEAD_DIM
    assert w == dil * C
    tq = min(DIL_TQ, L)
    H = DIL_HALF
    nh = L // H
    rq = tq // H
    main = lambda b, r, i: (b, i, r)
    prev = lambda b, r, i: (b, jnp.maximum(i * rq - 1, 0), r)
    nxt = lambda b, r, i: (b, jnp.minimum((i + 1) * rq, nh - 1), r)
    return pl.pallas_call(
        _dil_attn_kernel,
        grid=(B, dil, L // tq),
        in_specs=[
            pl.BlockSpec((None, tq, C), main),
            pl.BlockSpec((None, tq, C), main),
            pl.BlockSpec((None, H, C), prev),
            pl.BlockSpec((None, H, C), nxt),
            pl.BlockSpec((None, tq, C), main),
            pl.BlockSpec((None, H, C), prev),
            pl.BlockSpec((None, H, C), nxt),
        ],
        out_specs=[
            pl.BlockSpec((None, tq, C), main),
            pl.BlockSpec((None, tq, C), main),
        ],
        out_shape=[
            jax.ShapeDtypeStruct((B, L, dil * C), BF16),
            jax.ShapeDtypeStruct((B, L, dil * C), F32),
        ],
        scratch_shapes=[
            pltpu.VMEM((tq + 2 * H, C), BF16),
            pltpu.VMEM((tq + 2 * H, C), BF16),
        ],
        compiler_params=_cparams(("parallel", "parallel", "parallel")),
        name=f"dil_attn_{dil}",
    )(q, k, k, k, v, v, v)


def _rope_tables(S):
    inv = 1.0 / (ROPE_THETA ** (jnp.arange(0, HEAD_DIM, 2, dtype=F32) / HEAD_DIM))
    ang = jnp.arange(S, dtype=F32)[:, None] * inv[None, :]
    cos, sin = jnp.cos(ang), jnp.sin(ang)
    ck = jnp.concatenate([cos] * 4, axis=1)
    sk = jnp.concatenate([-sin, -sin, sin, sin], axis=1)
    ct = jnp.concatenate([cos.T, cos.T], axis=0)
    st = jnp.concatenate([sin.T, sin.T], axis=0)
    return ck, sk, ct, st


def _prep_ffn(w_up, conv_w, conv_b, w_down):
    nc = D_FF // FF_CHUNK
    D = w_up.shape[0]

    def chunk_cols(a):
        g = a[..., :D_FF].reshape(a.shape[:-1] + (nc, FF_CHUNK))
        u = a[..., D_FF:].reshape(a.shape[:-1] + (nc, FF_CHUNK))
        return jnp.moveaxis(jnp.concatenate([g, u], axis=-1), -2, 0)

    return (chunk_cols(w_up).astype(BF16), chunk_cols(conv_w), chunk_cols(conv_b[None, :]),
            w_down.reshape(nc, FF_CHUNK, D).astype(BF16))


def _trunk(x, p):
    B, S, D = x.shape
    ck, sk, ct, st = _rope_tables(S)
    row = lambda a: a.reshape(1, -1)
    for i in range(DEPTH):
        j = i // N_MIXERS
        if i % N_MIXERS == 0:
            lambda_init = 0.8 - 0.6 * math.exp(-0.3 * i)
            n = DA_HEADS * PAIR
            perm = _rope_perm(n)
            w_in = p['da_w_in'][j]
            wk = w_in[:, n:2 * n][:, perm].astype(BF16)
            wqv_t = jnp.concatenate([w_in[:, :n][:, perm], w_in[:, 2 * n:]], axis=1).T.astype(BF16)
            k, qt, vt = _qkv_a(x, row(p['norm_mix_pre'][i]), wk, wqv_t, ck, sk, ct, st)
            o = _diff_attn(qt, k, vt, row(p['da_lambda_q1'][j]), row(p['da_lambda_k1'][j]),
                           row(p['da_lambda_q2'][j]), row(p['da_lambda_k2'][j]),
                           row(p['da_subln'][j]), lambda_init)
            x = _out_proj(o, p['da_w_out'][j].astype(BF16), row(p['norm_mix_post'][i]), x)
        else:
            G = len(DL_PATTERNS)
            nq = G * DL_HEADS * HEAD_DIM
            perm = _rope_perm(nq)
            w_in = p['dl_w_in'][j]
            w = jnp.concatenate([w_in[:, :nq][:, perm], w_in[:, nq:2 * nq][:, perm], w_in[:, 2 * nq:]],
                                axis=1).astype(BF16)
            qkv = _qkv_b(x, row(p['norm_mix_pre'][i]), w, ck, sk)
            os, es = [], []
            for g, (window, dil) in enumerate(DL_PATTERNS):
                assert window // (2 * dil) == DIL_HALF
                o_g, e_g = _dil_attn(qkv[g], qkv[G + g], qkv[2 * G + g], dil)
                os.append(o_g)
                es.append(e_g)
            x = _combine_out_proj(os, es, p['dl_w_out'][j].astype(BF16), row(p['norm_mix_post'][i]), x)
        wup, cw, cb, wdn = _prep_ffn(p['ffn_w_up'][i], p['ffn_conv_w'][i], p['ffn_conv_b'][i], p['ffn_w_down'][i])
        x = _ffn(x, row(p['norm_ffn_pre'][i]), wup, cw, cb, wdn, row(p['norm_ffn_post'][i]))
    return x


def kernel(x_prompt, x_sample, norm_mix_pre, norm_mix_post, norm_ffn_pre, norm_ffn_post, da_w_in, da_w_out, da_lambda_q1, da_lambda_k1, da_lambda_q2, da_lambda_k2, da_subln, dl_w_in, dl_w_out, ffn_w_up, ffn_conv_w, ffn_conv_b, ffn_w_down):
    p = dict(norm_mix_pre=norm_mix_pre, norm_mix_post=norm_mix_post, norm_ffn_pre=norm_ffn_pre,
             norm_ffn_post=norm_ffn_post, da_w_in=da_w_in, da_w_out=da_w_out, da_lambda_q1=da_lambda_q1,
             da_lambda_k1=da_lambda_k1, da_lambda_q2=da_lambda_q2, da_lambda_k2=da_lambda_k2,
             da_subln=da_subln, dl_w_in=dl_w_in, dl_w_out=dl_w_out, ffn_w_up=ffn_w_up,
             ffn_conv_w=ffn_conv_w, ffn_conv_b=ffn_conv_b, ffn_w_down=ffn_w_down)
    return (_trunk(x_prompt, p), _trunk(x_sample, p))
```

```python
import functools
import math

import numpy as np
import jax
import jax.numpy as jnp
from jax import lax
from jax.experimental import pallas as pl
from jax.experimental.pallas import tpu as pltpu

D_MODEL = 1024
DEPTH = 2
N_MIXERS = 2
DA_HEADS = 8
HEAD_DIM = 64
DL_HEADS = 16
DL_PATTERNS = ((128, 1), (512, 4), (2048, 16))
D_FF = 2816
ROPE_THETA = 10000.0
NORM_EPS = 1e-6
NEG = -1e30

LANES = 128
PAIR = 2 * HEAD_DIM
VMEM_LIMIT = 56 * 1024 * 1024
Q_SCALE = (HEAD_DIM ** -0.5) * math.log2(math.e)

TM = 512
TQ = 512
TK = 512
KV_UNROLL = 8
FF_UNROLL = 4
Q_TILES_PER_STEP = 4
FF_CHUNK = 256
DIL_TQ = 1024
DIL_SUB = 128
DIL_HALF = 64
VT_ROWS = PAIR + 16

BF16 = jnp.bfloat16
F32 = jnp.float32


def _cparams(sem):
    return pltpu.CompilerParams(dimension_semantics=sem, vmem_limit_bytes=VMEM_LIMIT)


def _const_spec(shape):
    nd = len(shape)
    return pl.BlockSpec(shape, lambda *_: (0,) * nd, pipeline_mode=pl.Buffered(1))


def _rms(x, g):
    ms = jnp.mean(x * x, axis=-1, keepdims=True)
    return x * lax.rsqrt(ms + NORM_EPS) * g


def _rope_perm(n):
    return np.arange(n).reshape(n // PAIR, 2, 2, HEAD_DIM // 2).transpose(0, 2, 1, 3).reshape(-1)


def _qkv_a_kernel(x_ref, g_ref, wk_ref, wqv_ref, ck_ref, sk_ref, ct_ref, st_ref,
                  k_ref, qt_ref, vt_ref):
    xn = _rms(x_ref[...], g_ref[...]).astype(BF16)
    k = jnp.dot(xn, wk_ref[...], preferred_element_type=F32)
    ck = ck_ref[...]
    sk = sk_ref[...]
    for h in range(DA_HEADS):
        kh = k[:, h * PAIR:(h + 1) * PAIR]
        k_ref[:, h * PAIR:(h + 1) * PAIR] = (kh * ck + pltpu.roll(kh, HEAD_DIM, 1) * sk).astype(BF16)
    qvt = lax.dot_general(wqv_ref[...], xn, (((1,), (1,)), ((), ())), preferred_element_type=F32)
    ct = ct_ref[...]
    st = st_ref[...]
    for h in range(DA_HEADS):
        x1 = qvt[h * PAIR:h * PAIR + HEAD_DIM]
        x2 = qvt[h * PAIR + HEAD_DIM:(h + 1) * PAIR]
        qt_ref[h * PAIR:h * PAIR + HEAD_DIM, :] = ((x1 * ct - x2 * st) * Q_SCALE).astype(BF16)
        qt_ref[h * PAIR + HEAD_DIM:(h + 1) * PAIR, :] = ((x1 * st + x2 * ct) * Q_SCALE).astype(BF16)
    ones = jnp.ones((VT_ROWS - PAIR, vt_ref.shape[1]), BF16)
    for h in range(DA_HEADS):
        vt_ref[h * VT_ROWS:h * VT_ROWS + PAIR, :] = qvt[D_MODEL + h * PAIR:D_MODEL + (h + 1) * PAIR].astype(BF16)
        vt_ref[h * VT_ROWS + PAIR:(h + 1) * VT_ROWS, :] = ones


def _qkv_a(x, g, wk, wqv_t, ck, sk, ct, st):
    B, S, D = x.shape
    tm = min(TM, S)
    nqk = DA_HEADS * PAIR
    nvt = DA_HEADS * VT_ROWS
    return pl.pallas_call(
        _qkv_a_kernel,
        grid=(B, S // tm),
        in_specs=[
            pl.BlockSpec((None, tm, D), lambda b, i: (b, i, 0)),
            _const_spec((1, D)),
            _const_spec((D, nqk)),
            _const_spec((2 * nqk, D)),
            pl.BlockSpec((tm, PAIR), lambda b, i: (i, 0)),
            pl.BlockSpec((tm, PAIR), lambda b, i: (i, 0)),
            pl.BlockSpec((HEAD_DIM, tm), lambda b, i: (0, i)),
            pl.BlockSpec((HEAD_DIM, tm), lambda b, i: (0, i)),
        ],
        out_specs=[
            pl.BlockSpec((None, tm, nqk), lambda b, i: (b, i, 0)),
            pl.BlockSpec((None, nqk, tm), lambda b, i: (b, 0, i)),
            pl.BlockSpec((None, nvt, tm), lambda b, i: (b, 0, i)),
        ],
        out_shape=[
            jax.ShapeDtypeStruct((B, S, nqk), BF16),
            jax.ShapeDtypeStruct((B, nqk, S), BF16),
            jax.ShapeDtypeStruct((B, nvt, S), BF16),
        ],
        compiler_params=_cparams(("parallel", "parallel")),
        name="qkv_a",
    )(x, g, wk, wqv_t, ck, sk, ct, st)


def _diff_attn_kernel(qt_ref, k_ref, vt_ref, lq1_ref, lk1_ref, lq2_ref, lk2_ref, g_ref, o_ref,
                      *scratch, tq, tk, lambda_init):
    S = k_ref.shape[0]
    n = S // tk
    lam = (jnp.exp(jnp.sum(lq1_ref[...] * lk1_ref[...], axis=-1, keepdims=True))
           - jnp.exp(jnp.sum(lq2_ref[...] * lk2_ref[...], axis=-1, keepdims=True)) + lambda_init)

    unroll = KV_UNROLL
    while unroll > 2 and (n - 1) // unroll < 3:
        unroll //= 2
    trips = (n - 1) // unroll

    def one_tile(t, sa_ref, sb_ref, m_ref, acc_ref):
        qt = qt_ref[:, t * tq:(t + 1) * tq]
        row = lax.broadcasted_iota(jnp.int32, qt.shape, 0)
        comp0 = (row % HEAD_DIM) < (HEAD_DIM // 2)
        zero = jnp.zeros_like(qt)
        qts = (jnp.where(comp0, qt, zero), jnp.where(comp0, zero, qt))

        m_ref[...] = jnp.full(m_ref.shape, NEG, F32)
        acc_ref[...] = jnp.zeros(acc_ref.shape, F32)

        def scores(j, s_ref):
            kb = k_ref[pl.ds(pl.multiple_of(j * tk, tk), tk), :]
            for c in range(2):
                s_ref[c] = jnp.dot(kb, qts[c], preferred_element_type=F32)

        def softmax_pv(j, s_ref):
            vtb = vt_ref[:, pl.ds(pl.multiple_of(j * tk, tk), tk)]
            for c in range(2):
                s = s_ref[c]
                m_old = m_ref[c:c + 1, :]
                m_new = jnp.maximum(m_old, jnp.max(s, axis=0, keepdims=True))
                alpha = jnp.exp2(m_old - m_new)
                p = jnp.exp2(s - m_new).astype(BF16)
                acc_ref[c] = alpha * acc_ref[c] + jnp.dot(vtb, p, preferred_element_type=F32)
                m_ref[c:c + 1, :] = m_new

        bufs = (sa_ref, sb_ref)
        scores(0, sa_ref)

        def body(jj, carry):
            j = unroll * jj
            for u in range(unroll):
                scores(j + u + 1, bufs[(u + 1) % 2])
                softmax_pv(j + u, bufs[u % 2])
            return carry

        lax.fori_loop(0, trips, body, 0)
        done = trips * unroll
        for u in range(n - done):
            if done + u + 1 < n:
                scores(done + u + 1, bufs[(u + 1) % 2])
            softmax_pv(done + u, bufs[u % 2])

        o0 = acc_ref[0, :PAIR, :] * (1.0 / acc_ref[0, PAIR:PAIR + 1, :])
        o1 = acc_ref[1, :PAIR, :] * (1.0 / acc_ref[1, PAIR:PAIR + 1, :])
        ot = o0 - lam * o1
        ms = jnp.mean(ot * ot, axis=0, keepdims=True)
        on = (ot * lax.rsqrt(ms + NORM_EPS)).T
        o_ref[t * tq:(t + 1) * tq, :] = (on * g_ref[...] * (1.0 - lambda_init)).astype(o_ref.dtype)

    for t in range(len(scratch) // 4):
        one_tile(t, *scratch[4 * t:4 * t + 4])


def _diff_attn(qt, k, vt, lq1, lk1, lq2, lk2, subln_g, lambda_init):
    B, nqk, S = qt.shape
    tq = min(TQ, S)
    tk = min(TK, S)
    nt = Q_TILES_PER_STEP if S % (Q_TILES_PER_STEP * tq) == 0 else 1
    lam_spec = _const_spec((1, HEAD_DIM))
    tile_scratch = [
        pltpu.VMEM((2, tk, tq), F32),
        pltpu.VMEM((2, tk, tq), F32),
        pltpu.VMEM((8, tq), F32),
        pltpu.VMEM((2, VT_ROWS, tq), F32),
    ]
    return pl.pallas_call(
        functools.partial(_diff_attn_kernel, tq=tq, tk=tk, lambda_init=lambda_init),
        grid=(B, DA_HEADS, S // (nt * tq)),
        in_specs=[
            pl.BlockSpec((None, PAIR, nt * tq), lambda b, h, i: (b, h, i)),
            pl.BlockSpec((None, S, PAIR), lambda b, h, i: (b, 0, h)),
            pl.BlockSpec((None, VT_ROWS, S), lambda b, h, i: (b, h, 0)),
            lam_spec, lam_spec, lam_spec, lam_spec,
            _const_spec((1, PAIR)),
        ],
        out_specs=pl.BlockSpec((None, nt * tq, PAIR), lambda b, h, i: (b, i, h)),
        out_shape=jax.ShapeDtypeStruct((B, S, nqk), BF16),
        scratch_shapes=tile_scratch * nt,
        compiler_params=_cparams(("parallel", "parallel", "arbitrary")),
        name="diff_attn",
    )(qt, k, vt, lq1, lk1, lq2, lk2, subln_g)


def _out_proj_kernel(o_ref, w_ref, g_ref, x_ref, y_ref):
    h = jnp.dot(o_ref[...], w_ref[...], preferred_element_type=F32)
    y_ref[...] = x_ref[...] + _rms(h, g_ref[...])


def _out_proj(o, w, g, x):
    B, S, D = x.shape
    tm = min(TM, S)
    tok = pl.BlockSpec((None, tm, D), lambda b, i: (b, i, 0))
    return pl.pallas_call(
        _out_proj_kernel,
        grid=(B, S // tm),
        in_specs=[tok, _const_spec((D, D)), _const_spec((1, D)), tok],
        out_specs=tok,
        out_shape=jax.ShapeDtypeStruct((B, S, D), F32),
        compiler_params=_cparams(("parallel", "parallel")),
        name="out_proj",
    )(o, w, g, x)


def _combine_out_proj_kernel(*refs):
    G = len(DL_PATTERNS)
    o_refs, e_refs = refs[:G], refs[G:2 * G]
    w_ref, g_ref, x_ref, y_ref = refs[2 * G:2 * G + 4]
    ocat_ref = refs[2 * G + 4]
    staged = [g for g, (_, dil) in enumerate(DL_PATTERNS) if dil > 1]
    buf_refs = {}
    for n, g in enumerate(staged):
        buf_refs[g] = refs[2 * G + 5 + n]
        buf_refs[G + g] = refs[2 * G + 5 + len(staged) + n]
    tm, C = x_ref.shape
    n_slab = C // LANES

    for g, (_, dil) in enumerate(DL_PATTERNS):
        if dil == 1:
            continue
        rows = tm // dil
        for ref, buf_ref in ((o_refs[g], buf_refs[g]), (e_refs[g], buf_refs[G + g])):
            for r in range(dil):
                for s in range(n_slab):
                    buf_ref[s, pl.ds(r, rows, stride=dil), :] = (
                        ref[:, r * C + s * LANES:r * C + (s + 1) * LANES].astype(F32))

    def slab(g, s, kind):
        if DL_PATTERNS[g][1] == 1:
            ref = (o_refs, e_refs)[kind][g]
            return ref[:, s * LANES:(s + 1) * LANES].astype(F32)
        return buf_refs[kind * G + g][s]

    for s in range(n_slab):
        es = [slab(g, s, 1) for g in range(G)]
        m = functools.reduce(jnp.maximum, es)
        ws = [jnp.exp2(e - m) for e in es]
        o = sum(w * slab(g, s, 0) for g, w in enumerate(ws)) * (1.0 / sum(ws))
        ocat_ref[:, s * LANES:(s + 1) * LANES] = o.astype(BF16)
    h = jnp.dot(ocat_ref[...], w_ref[...], preferred_element_type=F32)
    y_ref[...] = x_ref[...] + _rms(h, g_ref[...])


def _combine_out_proj(os, es, w, g, x):
    B, S, D = x.shape
    tm = min(TM, S)
    tok = pl.BlockSpec((None, tm, D), lambda b, i: (b, i, 0))
    views = [pl.BlockSpec((None, tm // dil, dil * D), lambda b, i: (b, i, 0)) for _, dil in DL_PATTERNS]
    return pl.pallas_call(
        _combine_out_proj_kernel,
        grid=(B, S // tm),
        in_specs=views + views + [_const_spec((D, D)), _const_spec((1, D)), tok],
        out_specs=tok,
        out_shape=jax.ShapeDtypeStruct((B, S, D), F32),
        scratch_shapes=[pltpu.VMEM((tm, D), BF16)]
        + [pltpu.VMEM((D // LANES, tm, LANES), F32)] * (2 * sum(dil > 1 for _, dil in DL_PATTERNS)),
        compiler_params=_cparams(("parallel", "parallel")),
        name="combine_out_proj",
    )(*os, *es, w, g, x)


def _ffn_kernel(x_ref, xp_ref, xn_ref, gpre_ref, wup_ref, cw_ref, cb_ref, wdn_ref, gpost_ref, y_ref,
                ha_ref, hb_ref, acc_ref):
    i = pl.program_id(1)
    last = pl.num_programs(1) - 1
    tm = x_ref.shape[0]
    x = x_ref[...]
    gpre = gpre_ref[...]
    keep_prev = (i > 0).astype(F32)
    keep_next = (i < last).astype(F32)
    xe = jnp.concatenate([_rms(xp_ref[...], gpre) * keep_prev, _rms(x, gpre),
                          _rms(xn_ref[...], gpre) * keep_next], axis=0).astype(BF16)
    acc_ref[...] = jnp.zeros(acc_ref.shape, F32)
    n_chunks = wup_ref.shape[0]

    n_slab = 2 * FF_CHUNK // LANES

    def up_proj(c, h_ref):
        h = jnp.dot(xe, wup_ref[c], preferred_element_type=F32)
        for s in range(n_slab):
            h_ref[s] = h[:, s * LANES:(s + 1) * LANES]

    def conv_act_down(c, h_ref):
        cw = cw_ref[c]
        cb = cb_ref[c]
        ys = []
        for s in range(n_slab):
            cols = slice(s * LANES, (s + 1) * LANES)
            ys.append(h_ref[s, pl.ds(7, tm), :] * cw[0:1, cols] + h_ref[s, pl.ds(8, tm), :] * cw[1:2, cols]
                      + h_ref[s, pl.ds(9, tm), :] * cw[2:3, cols] + cb[:, cols])
        gate = jnp.concatenate(ys[:n_slab // 2], axis=1)
        up = jnp.concatenate(ys[n_slab // 2:], axis=1)
        cdf = 0.5 * (1.0 + jnp.tanh(math.sqrt(2.0 / math.pi) * (gate + 0.044715 * (gate * gate * gate))))
        act = (gate * cdf * up).astype(BF16)
        acc_ref[...] += jnp.dot(act, wdn_ref[c], preferred_element_type=F32)

    bufs = (ha_ref, hb_ref)
    up_proj(0, ha_ref)

    def body(cc, carry):
        c = FF_UNROLL * cc
        for u in range(FF_UNROLL):
            up_proj(c + u + 1, bufs[(u + 1) % 2])
            conv_act_down(c + u, bufs[u % 2])
        return carry

    trips = (n_chunks - 1) // FF_UNROLL
    lax.fori_loop(0, trips, body, 0)
    done = trips * FF_UNROLL
    for u in range(n_chunks - done):
        if done + u + 1 < n_chunks:
            up_proj(done + u + 1, bufs[(u + 1) % 2])
        conv_act_down(done + u, bufs[u % 2])
    y_ref[...] = x + _rms(acc_ref[...], gpost_ref[...])


def _ffn(x, gpre, wup, cw, cb, wdn, gpost):
    B, S, D = x.shape
    tm = min(TM, S)
    nt8 = S // 8
    r8 = tm // 8
    nc = wup.shape[0]
    return pl.pallas_call(
        _ffn_kernel,
        grid=(B, S // tm),
        in_specs=[
            pl.BlockSpec((None, tm, D), lambda b, i: (b, i, 0)),
            pl.BlockSpec((None, 8, D), lambda b, i: (b, jnp.maximum(i * r8 - 1, 0), 0)),
            pl.BlockSpec((None, 8, D), lambda b, i: (b, jnp.minimum((i + 1) * r8, nt8 - 1), 0)),
            _const_spec((1, D)),
            _const_spec((nc, D, 2 * FF_CHUNK)),
            _const_spec((nc, 3, 2 * FF_CHUNK)),
            _const_spec((nc, 1, 2 * FF_CHUNK)),
            _const_spec((nc, FF_CHUNK, D)),
            _const_spec((1, D)),
        ],
        out_specs=pl.BlockSpec((None, tm, D), lambda b, i: (b, i, 0)),
        out_shape=jax.ShapeDtypeStruct((B, S, D), F32),
        scratch_shapes=[
            pltpu.VMEM((2 * FF_CHUNK // LANES, tm + 16, LANES), F32),
            pltpu.VMEM((2 * FF_CHUNK // LANES, tm + 16, LANES), F32),
            pltpu.VMEM((tm, D), F32),
        ],
        compiler_params=_cparams(("parallel", "parallel")),
        name="conv_ffn",
    )(x, x, x, gpre, wup, cw, cb, wdn, gpost)


def _qkv_b_kernel(x_ref, g_ref, w_ref, ck_ref, sk_ref, *refs):
    G = len(DL_PATTERNS)
    q_refs, k_refs, v_refs = refs[:G], refs[G:2 * G], refs[2 * G:3 * G]
    ybuf_ref = refs[3 * G]
    tm = x_ref.shape[0]
    C = D_MODEL
    n_slab = C // LANES
    xn = _rms(x_ref[...], g_ref[...]).astype(BF16)

    def emit(slabs, dsts):
        if any(dil > 1 for _, dil in dsts):
            for s in range(n_slab):
                ybuf_ref[s] = slabs[s]
        for ref, dil in dsts:
            if dil == 1:
                for s in range(n_slab):
                    ref[:, s * LANES:(s + 1) * LANES] = slabs[s].astype(BF16)
            else:
                rows = tm // dil
                for r in range(dil):
                    for s in range(n_slab):
                        ref[:, r * C + s * LANES:r * C + (s + 1) * LANES] = (
                            ybuf_ref[s, pl.ds(r, rows, stride=dil), :].astype(BF16))

    for c in range(2 * G + 1):
        y = jnp.dot(xn, w_ref[:, c * C:(c + 1) * C], preferred_element_type=F32)
        slabs = [y[:, s * LANES:(s + 1) * LANES] for s in range(n_slab)]
        if c < 2 * G:
            scale = Q_SCALE if c < G else 1.0
            slabs = [yh * ck_ref[...] + pltpu.roll(yh, HEAD_DIM, 1) * sk_ref[...] for yh in slabs]
            if scale != 1.0:
                slabs = [yh * scale for yh in slabs]
            g = c % G
            emit(slabs, [((q_refs if c < G else k_refs)[g], DL_PATTERNS[g][1])])
        else:
            emit(slabs, [(v_refs[g], DL_PATTERNS[g][1]) for g in range(G)])


def _qkv_b(x, g, w, ck, sk):
    B, S, D = x.shape
    tm = min(TM, S)
    G = len(DL_PATTERNS)
    C = DL_HEADS * HEAD_DIM
    out_specs, out_shape = [], []
    for _ in range(3):
        for _, dil in DL_PATTERNS:
            out_specs.append(pl.BlockSpec((None, tm // dil, dil * C), lambda b, i: (b, i, 0)))
            out_shape.append(jax.ShapeDtypeStruct((B, S // dil, dil * C), BF16))
    return pl.pallas_call(
        _qkv_b_kernel,
        grid=(B, S // tm),
        in_specs=[
            pl.BlockSpec((None, tm, D), lambda b, i: (b, i, 0)),
            _const_spec((1, D)),
            _const_spec((D, (2 * G + 1) * C)),
            pl.BlockSpec((tm, PAIR), lambda b, i: (i, 0)),
            pl.BlockSpec((tm, PAIR), lambda b, i: (i, 0)),
        ],
        out_specs=out_specs,
        out_shape=out_shape,
        scratch_shapes=[pltpu.VMEM((C // LANES, tm, LANES), F32)],
        compiler_params=_cparams(("parallel", "parallel")),
        name="qkv_b",
    )(x, g, w, ck, sk)


def _dil_attn_kernel(q_ref, k_ref, kp_ref, kn_ref, v_ref, vp_ref, vn_ref, o_ref, e_ref, kx_ref, vx_ref):
    i = pl.program_id(2)
    last = pl.num_programs(2) - 1
    tq = q_ref.shape[0]
    H = DIL_HALF
    kx_ref[0:H, :] = kp_ref[...]
    kx_ref[H:H + tq, :] = k_ref[...]
    kx_ref[H + tq:, :] = kn_ref[...]
    vx_ref[0:H, :] = vp_ref[...]
    vx_ref[H:H + tq, :] = v_ref[...]
    vx_ref[H + tq:, :] = vn_ref[...]

    sub = DIL_SUB
    win = sub + 2 * H
    n_sub = tq // sub
    jq = lax.broadcasted_iota(jnp.int32, (sub, win), 0)
    ck = lax.broadcasted_iota(jnp.int32, (sub, win), 1)
    band = (ck >= jq) & (ck <= jq + 2 * H)
    band_first = band & (ck >= jnp.where(i > 0, 0, H))
    band_last = band & (ck < jnp.where(i < last, win, win - H))
    lane = lax.broadcasted_iota(jnp.int32, (sub, PAIR), 1)
    head_a = (lane % HEAD_DIM) < (HEAD_DIM // 2)
    out_a = lane < HEAD_DIM

    for s in range(n_sub):
        valid = band_first if s == 0 else (band_last if s == n_sub - 1 else band)
        if n_sub == 1:
            valid = band_first & band_last
        for hp in range(q_ref.shape[1] // PAIR):
            cols = slice(hp * PAIR, (hp + 1) * PAIR)
            qp = q_ref[s * sub:(s + 1) * sub, cols]
            kw = kx_ref[s * sub:s * sub + win, cols]
            vw = vx_ref[s * sub:s * sub + win, cols]
            zero = jnp.zeros_like(qp)
            q2 = jnp.concatenate([jnp.where(head_a, qp, zero), jnp.where(head_a, zero, qp)], axis=0)
            sc2 = lax.dot_general(q2, kw, (((1,), (1,)), ((), ())), preferred_element_type=F32)
            ms, ls, ps = [], [], []
            for hh in range(2):
                sc = jnp.where(valid, sc2[hh * sub:(hh + 1) * sub], NEG)
                m = jnp.max(sc, axis=-1, keepdims=True)
                p = jnp.exp2(sc - m)
                ms.append(m)
                ls.append(jnp.sum(p, axis=-1, keepdims=True))
                ps.append(p.astype(BF16))
            o2 = jnp.dot(jnp.concatenate(ps, axis=0), vw, preferred_element_type=F32)
            l = jnp.where(out_a, ls[0], ls[1])
            o = jnp.where(out_a, o2[:sub], o2[sub:]) * (1.0 / l)
            o_ref[s * sub:(s + 1) * sub, cols] = o.astype(o_ref.dtype)
            e_ref[s * sub:(s + 1) * sub, cols] = jnp.where(out_a, ms[0], ms[1]) + jnp.log2(l)


def _dil_attn(q, k, v, dil):
    B, L, w = q.shape
    C = DL_HEADS * HEAD_DIM
    assert w == dil * C
    tq = min(DIL_TQ, L)
    H = DIL_HALF
    nh = L // H
    rq = tq // H
    main = lambda b, r, i: (b, i, r)
    prev = lambda b, r, i: (b, jnp.maximum(i * rq - 1, 0), r)
    nxt = lambda b, r, i: (b, jnp.minimum((i + 1) * rq, nh - 1), r)
    return pl.pallas_call(
        _dil_attn_kernel,
        grid=(B, dil, L // tq),
        in_specs=[
            pl.BlockSpec((None, tq, C), main),
            pl.BlockSpec((None, tq, C), main),
            pl.BlockSpec((None, H, C), prev),
            pl.BlockSpec((None, H, C), nxt),
            pl.BlockSpec((None, tq, C), main),
            pl.BlockSpec((None, H, C), prev),
            pl.BlockSpec((None, H, C), nxt),
        ],
        out_specs=[
            pl.BlockSpec((None, tq, C), main),
            pl.BlockSpec((None, tq, C), main),
        ],
        out_shape=[
            jax.ShapeDtypeStruct((B, L, dil * C), BF16),
            jax.ShapeDtypeStruct((B, L, dil * C), F32),
        ],
        scratch_shapes=[
            pltpu.VMEM((tq + 2 * H, C), BF16),
            pltpu.VMEM((tq + 2 * H, C), BF16),
        ],
        compiler_params=_cparams(("parallel", "parallel", "parallel")),
        name=f"dil_attn_{dil}",
    )(q, k, k, k, v, v, v)


def _rope_tables(S):
    inv = 1.0 / (ROPE_THETA ** (jnp.arange(0, HEAD_DIM, 2, dtype=F32) / HEAD_DIM))
    ang = jnp.arange(S, dtype=F32)[:, None] * inv[None, :]
    cos, sin = jnp.cos(ang), jnp.sin(ang)
    ck = jnp.concatenate([cos] * 4, axis=1)
    sk = jnp.concatenate([-sin, -sin, sin, sin], axis=1)
    ct = jnp.concatenate([cos.T, cos.T], axis=0)
    st = jnp.concatenate([sin.T, sin.T], axis=0)
    return ck, sk, ct, st


def _prep_ffn(w_up, conv_w, conv_b, w_down):
    nc = D_FF // FF_CHUNK
    D = w_up.shape[0]

    def chunk_cols(a):
        g = a[..., :D_FF].reshape(a.shape[:-1] + (nc, FF_CHUNK))
        u = a[..., D_FF:].reshape(a.shape[:-1] + (nc, FF_CHUNK))
        return jnp.moveaxis(jnp.concatenate([g, u], axis=-1), -2, 0)

    return (chunk_cols(w_up).astype(BF16), chunk_cols(conv_w), chunk_cols(conv_b[None, :]),
            w_down.reshape(nc, FF_CHUNK, D).astype(BF16))


def _trunk(x, p):
    B, S, D = x.shape
    ck, sk, ct, st = _rope_tables(S)
    row = lambda a: a.reshape(1, -1)
    for i in range(DEPTH):
        j = i // N_MIXERS
        if i % N_MIXERS == 0:
            lambda_init = 0.8 - 0.6 * math.exp(-0.3 * i)
            n = DA_HEADS * PAIR
            perm = _rope_perm(n)
            w_in = p['da_w_in'][j]
            wk = w_in[:, n:2 * n][:, perm].astype(BF16)
            wqv_t = jnp.concatenate([w_in[:, :n][:, perm], w_in[:, 2 * n:]], axis=1).T.astype(BF16)
            k, qt, vt = _qkv_a(x, row(p['norm_mix_pre'][i]), wk, wqv_t, ck, sk, ct, st)
            o = _diff_attn(qt, k, vt, row(p['da_lambda_q1'][j]), row(p['da_lambda_k1'][j]),
                           row(p['da_lambda_q2'][j]), row(p['da_lambda_k2'][j]),
                           row(p['da_subln'][j]), lambda_init)
            x = _out_proj(o, p['da_w_out'][j].astype(BF16), row(p['norm_mix_post'][i]), x)
        else:
            G = len(DL_PATTERNS)
            nq = G * DL_HEADS * HEAD_DIM
            perm = _rope_perm(nq)
            w_in = p['dl_w_in'][j]
            w = jnp.concatenate([w_in[:, :nq][:, perm], w_in[:, nq:2 * nq][:, perm], w_in[:, 2 * nq:]],
                                axis=1).astype(BF16)
            qkv = _qkv_b(x, row(p['norm_mix_pre'][i]), w, ck, sk)
            os, es = [], []
            for g, (window, dil) in enumerate(DL_PATTERNS):
                assert window // (2 * dil) == DIL_HALF
                o_g, e_g = _dil_attn(qkv[g], qkv[G + g], qkv[2 * G + g], dil)
                os.append(o_g)
                es.append(e_g)
            x = _combine_out_proj(os, es, p['dl_w_out'][j].astype(BF16), row(p['norm_mix_post'][i]), x)
        wup, cw, cb, wdn = _prep_ffn(p['ffn_w_up'][i], p['ffn_conv_w'][i], p['ffn_conv_b'][i], p['ffn_w_down'][i])
        x = _ffn(x, row(p['norm_ffn_pre'][i]), wup, cw, cb, wdn, row(p['norm_ffn_post'][i]))
    return x


def kernel(x_prompt, x_sample, norm_mix_pre, norm_mix_post, norm_ffn_pre, norm_ffn_post, da_w_in, da_w_out, da_lambda_q1, da_lambda_k1, da_lambda_q2, da_lambda_k2, da_subln, dl_w_in, dl_w_out, ffn_w_up, ffn_conv_w, ffn_conv_b, ffn_w_down):
    p = dict(norm_mix_pre=norm_mix_pre, norm_mix_post=norm_mix_post, norm_ffn_pre=norm_ffn_pre,
             norm_ffn_post=norm_ffn_post, da_w_in=da_w_in, da_w_out=da_w_out, da_lambda_q1=da_lambda_q1,
             da_lambda_k1=da_lambda_k1, da_lambda_q2=da_lambda_q2, da_lambda_k2=da_lambda_k2,
             da_subln=da_subln, dl_w_in=dl_w_in, dl_w_out=dl_w_out, ffn_w_up=ffn_w_up,
             ffn_conv_w=ffn_conv_w, ffn_conv_b=ffn_conv_b, ffn_w_down=ffn_w_down)
    return (_trunk(x_prompt, p), _trunk(x_sample, p))
```
